```python
import math
import jax, jax.numpy as jnp
from jax import lax
import numpy as np

D_MODEL = 1024
BATCH = 16
SEQ = 4096
DEPTH = 2

POOL_WIDTH = 256
POOL_GROUPS = 4
POOL_GROUP_DIM = POOL_WIDTH // POOL_GROUPS
POOL_WINDOWS = (2, 4, 8, 16)
ATTN_HEADS = 4
ATTN_QK_DIM = 64
ATTN_V_DIM = 2 * ATTN_QK_DIM
ATTN_QK_WIDTH = ATTN_HEADS * 2 * ATTN_QK_DIM
ATTN_WIDTH = ATTN_HEADS * ATTN_V_DIM
Q_BLOCK = 128
LRU_WIDTH = 256
LRU_BLOCKS = 4
LRU_BLOCK_DIM = LRU_WIDTH // LRU_BLOCKS
LRU_CONV_WIDTH = 4
LRU_C = 8.0
LRU_DIRECTIONS = 2
N_BRANCH = 3
D_FF = 2816
NORM_EPS = 1e-6
SUBLN_EPS = 1e-5
IN_SPLITS = (POOL_WIDTH, ATTN_QK_WIDTH, ATTN_QK_WIDTH, ATTN_WIDTH, LRU_WIDTH, LRU_WIDTH, N_BRANCH * D_MODEL)
IN_WIDTH = 256 + 512 + 512 + 512 + 256 + 256 + 3 * 1024

kernel_name = "hybrid_pool_diffattn_rglru_encoder"


def rms_norm(x, g, eps=NORM_EPS):
    xf = x.astype(jnp.float32)
    y = xf * lax.rsqrt(jnp.mean(xf * xf, axis=-1, keepdims=True) + eps)
    return (y * g.astype(jnp.float32)).astype(x.dtype)


def swiglu(h, w_in, w_out):
    gate, up = jnp.split(h @ w_in, 2, axis=-1)
    return (jax.nn.silu(gate) * up) @ w_out


def alibi_slopes(n_heads):
    start = 2.0 ** (-8.0 / n_heads)
    return np.array([start ** (i + 1) for i in range(n_heads)], dtype=np.float32)


def pool_mixer(p, pool_w, pool_scale):
    b, s, _ = p.shape
    pf = p.astype(jnp.float32)
    csum = jnp.concatenate([jnp.zeros((b, 1, POOL_WIDTH), jnp.float32), jnp.cumsum(pf, axis=1)], axis=1)
    t = jnp.arange(s)
    outs = []
    for g, w in enumerate(POOL_WINDOWS):
        lo = jnp.clip(t - w // 2, 0, s)
        hi = jnp.clip(t + w - w // 2, 0, s)
        sl = slice(g * POOL_GROUP_DIM, (g + 1) * POOL_GROUP_DIM)
        cg = csum[..., sl]
        win_sum = jnp.take(cg, hi, axis=1) - jnp.take(cg, lo, axis=1)
        count = (hi - lo).astype(jnp.float32)[None, :, None]
        outs.append(win_sum / count - pf[..., sl])
    mixed = jnp.stack(outs, axis=2)
    mixed = jnp.einsum('bsgc,gcd->bsgd', mixed, pool_w.astype(jnp.float32)).reshape(b, s, POOL_WIDTH)
    return (mixed * pool_scale.astype(jnp.float32)).astype(p.dtype)


def diff_attention(q, k, v, lam_params, subln_g, lam_init):
    b, s, _ = q.shape
    f32 = jnp.float32
    qf = q.reshape(b, s, ATTN_HEADS, 2, ATTN_QK_DIM).astype(f32) * (ATTN_QK_DIM ** -0.5)
    kf = k.reshape(b, s, ATTN_HEADS, 2, ATTN_QK_DIM).astype(f32)
    vf = v.reshape(b, s, ATTN_HEADS, ATTN_V_DIM).astype(f32)
    lp = lam_params.astype(f32)
    lam = jnp.exp(jnp.sum(lp[0] * lp[1])) - jnp.exp(jnp.sum(lp[2] * lp[3])) + lam_init
    slopes = jnp.asarray(alibi_slopes(ATTN_HEADS))
    kpos = jnp.arange(s)
    nq = s // Q_BLOCK
    q_blocks = qf.reshape(b, nq, Q_BLOCK, ATTN_HEADS, 2, ATTN_QK_DIM).transpose(1, 0, 2, 3, 4, 5)

    def block(args):
        qb, start = args
        qpos = start + jnp.arange(Q_BLOCK)
        dist = jnp.abs(qpos[:, None] - kpos[None, :]).astype(f32)
        bias = -slopes[:, None, None, None] * dist
        scores = jnp.einsum('bqhmd,bkhmd->bhmqk', qb, kf) + bias
        probs = jax.nn.softmax(scores, axis=-1)
        wts = probs[:, :, 0] - lam * probs[:, :, 1]
        return jnp.einsum('bhqk,bkhe->bqhe', wts, vf)

    o = lax.map(block, (q_blocks, jnp.arange(nq) * Q_BLOCK))
    o = o.transpose(1, 0, 2, 3, 4).reshape(b, s, ATTN_HEADS, ATTN_V_DIM)
    o = o * lax.rsqrt(jnp.mean(o * o, axis=-1, keepdims=True) + SUBLN_EPS) * subln_g.astype(f32)
    o = o * (1.0 - lam_init)
    return o.reshape(b, s, ATTN_WIDTH).astype(q.dtype)


def rg_lru(xf, w_a, b_a, w_x, b_x, lam, reverse):
    b, s, _ = xf.shape
    f32 = jnp.float32
    xb = xf.reshape(b, s, LRU_BLOCKS, LRU_BLOCK_DIM)
    r = jax.nn.sigmoid(jnp.einsum('bsgc,gcd->bsgd', xb, w_a.astype(f32)).reshape(b, s, LRU_WIDTH) + b_a.astype(f32))
    i = jax.nn.sigmoid(jnp.einsum('bsgc,gcd->bsgd', xb, w_x.astype(f32)).reshape(b, s, LRU_WIDTH) + b_x.astype(f32))
    log_a = -LRU_C * r * jax.nn.softplus(-lam.astype(f32))
    a = jnp.exp(log_a)
    u = jnp.sqrt(-jnp.expm1(2.0 * log_a)) * (i * xf)

    def combine(c1, c2):
        a1, b1 = c1
        a2, b2 = c2
        return a1 * a2, a2 * b1 + b2

    _, h = lax.associative_scan(combine, (a, u), axis=1, reverse=reverse)
    return h


def rglru_branch(lx, lg, conv_w, conv_b, w_a, b_a, w_x, b_x, lam):
    conv = lax.conv_general_dilated(
        lx, conv_w[:, None, :], window_strides=(1,),
        padding=[(LRU_CONV_WIDTH // 2, LRU_CONV_WIDTH - 1 - LRU_CONV_WIDTH // 2)],
        dimension_numbers=('NWC', 'WIO', 'NWC'), feature_group_count=LRU_WIDTH)
    xf = (conv + conv_b).astype(jnp.float32)
    h = (rg_lru(xf, w_a[0], b_a[0], w_x[0], b_x[0], lam[0], reverse=False)
         + rg_lru(xf, w_a[1], b_a[1], w_x[1], b_x[1], lam[1], reverse=True))
    return (jax.nn.gelu(lg.astype(jnp.float32)) * h).astype(lx.dtype)


def token_mixer(h, w_in, pool_w, pool_scale, attn_lambda, attn_subln, lru_conv_w, lru_conv_b,
                lru_w_a, lru_b_a, lru_w_x, lru_b_x, lru_lambda, w_branch_pool, w_branch_attn,
                w_branch_lru, merge_bias, w_out, lam_init):
    b, s, _ = h.shape
    proj = h @ w_in
    offsets = []
    acc = 0
    for w in IN_SPLITS[:-1]:
        acc += w
        offsets.append(acc)
    p, q, k, v, lx, lg, gate_logits = jnp.split(proj, offsets, axis=-1)
    y_pool = pool_mixer(p, pool_w, pool_scale) @ w_branch_pool
    y_attn = diff_attention(q, k, v, attn_lambda, attn_subln, lam_init) @ w_branch_attn
    y_lru = rglru_branch(lx, lg, lru_conv_w, lru_conv_b, lru_w_a, lru_b_a, lru_w_x, lru_b_x, lru_lambda) @ w_branch_lru
    gates = jax.nn.sigmoid(gate_logits.reshape(b, s, N_BRANCH, D_MODEL).astype(jnp.float32)
                           + merge_bias.astype(jnp.float32)).astype(h.dtype)
    merged = gates[:, :, 0] * y_pool + gates[:, :, 1] * y_attn + gates[:, :, 2] * y_lru
    return merged @ w_out


def setup_inputs(seed: int = 0) -> dict:
    key = jax.random.key(seed)
    ks = jax.random.split(key, 32)
    L, D, F = DEPTH, D_MODEL, D_FF
    f32 = jnp.float32

    def dense(k, shape, fan_in):
        return jax.random.normal(k, shape, f32) * (fan_in ** -0.5)

    def gain(k, shape):
        return 1.0 + 0.02 * jax.random.normal(k, shape, f32)

    def small(k, shape, scale=0.01):
        return scale * jax.random.normal(k, shape, f32)

    u = jax.random.uniform(ks[16], (L, LRU_DIRECTIONS, LRU_WIDTH), f32, 0.9, 0.999)
    a0 = u ** (1.0 / LRU_C)
    lru_lambda = jnp.log(a0) - jnp.log1p(-a0)

    return {
        "x": jax.random.normal(ks[0], (BATCH, SEQ, D), f32),
        "ffn1_norm": gain(ks[1], (L, D)),
        "ffn1_w_in": dense(ks[2], (L, D, 2 * F), D),
        "ffn1_w_out": dense(ks[3], (L, F, D), F),
        "mix_norm": gain(ks[4], (L, D)),
        "w_in": dense(ks[5], (L, D, IN_WIDTH), D),
        "pool_w": dense(ks[6], (L, POOL_GROUPS, POOL_GROUP_DIM, POOL_GROUP_DIM), POOL_GROUP_DIM),
        "pool_scale": 1.0 + 0.1 * jax.random.normal(ks[7], (L, POOL_WIDTH), f32),
        "attn_lambda": small(ks[8], (L, 4, ATTN_QK_DIM), 0.1),
        "attn_subln": gain(ks[9], (L, ATTN_V_DIM)),
        "lru_conv_w": dense(ks[10], (L, LRU_CONV_WIDTH, LRU_WIDTH), LRU_CONV_WIDTH),
        "lru_conv_b": small(ks[11], (L, LRU_WIDTH)),
        "lru_w_a": dense(ks[12], (L, LRU_DIRECTIONS, LRU_BLOCKS, LRU_BLOCK_DIM, LRU_BLOCK_DIM), LRU_BLOCK_DIM),
        "lru_b_a": small(ks[13], (L, LRU_DIRECTIONS, LRU_WIDTH)),
        "lru_w_x": dense(ks[14], (L, LRU_DIRECTIONS, LRU_BLOCKS, LRU_BLOCK_DIM, LRU_BLOCK_DIM), LRU_BLOCK_DIM),
        "lru_b_x": small(ks[15], (L, LRU_DIRECTIONS, LRU_WIDTH)),
        "lru_lambda": lru_lambda,
        "w_branch_pool": dense(ks[17], (L, POOL_WIDTH, D), POOL_WIDTH),
        "w_branch_attn": dense(ks[18], (L, ATTN_WIDTH, D), ATTN_WIDTH),
        "w_branch_lru": dense(ks[19], (L, LRU_WIDTH, D), LRU_WIDTH),
        "merge_bias": small(ks[20], (L, N_BRANCH, D)),
        "w_out": dense(ks[21], (L, D, D), D),
        "ffn2_norm": gain(ks[22], (L, D)),
        "ffn2_w_in": dense(ks[23], (L, D, 2 * F), D),
        "ffn2_w_out": dense(ks[24], (L, F, D), F),
        "final_norm": gain(ks[25], (D,)),
    }


def reference(x, ffn1_norm, ffn1_w_in, ffn1_w_out, mix_norm, w_in, pool_w, pool_scale,
              attn_lambda, attn_subln, lru_conv_w, lru_conv_b, lru_w_a, lru_b_a, lru_w_x,
              lru_b_x, lru_lambda, w_branch_pool, w_branch_attn, w_branch_lru, merge_bias,
              w_out, ffn2_norm, ffn2_w_in, ffn2_w_out, final_norm):
    for l in range(DEPTH):
        lam_init = 0.8 - 0.6 * math.exp(-0.3 * l)
        x = x + 0.5 * swiglu(rms_norm(x, ffn1_norm[l]), ffn1_w_in[l], ffn1_w_out[l])
        x = x + token_mixer(rms_norm(x, mix_norm[l]), w_in[l], pool_w[l], pool_scale[l],
                            attn_lambda[l], attn_subln[l], lru_conv_w[l], lru_conv_b[l],
                            lru_w_a[l], lru_b_a[l], lru_w_x[l], lru_b_x[l], lru_lambda[l],
                            w_branch_pool[l], w_branch_attn[l], w_branch_lru[l],
                            merge_bias[l], w_out[l], lam_init)
        x = x + 0.5 * swiglu(rms_norm(x, ffn2_norm[l]), ffn2_w_in[l], ffn2_w_out[l])
    return rms_norm(x, final_norm)
```

```python
import functools
import math

import jax
import jax.numpy as jnp
from jax import lax
from jax.experimental import pallas as pl
from jax.experimental.pallas import tpu as pltpu

F32 = jnp.float32
BF16 = jnp.bfloat16

NORM_EPS = 1e-6
SUBLN_EPS = 1e-5
LRU_C = 8.0

POOL_WINDOWS = (2, 4, 8, 16)
POOL_WIDTH = 256
ATTN_HEADS = 4
ATTN_QK_DIM = 64
ATTN_V_DIM = 128
ATTN_WIDTH = ATTN_HEADS * ATTN_V_DIM
LRU_WIDTH = 256
LRU_CONV_WIDTH = 4

LANES = 128
MXU_COLS = 256
VMEM_LIMIT_BYTES = 52 * 1024 * 1024


def _params(*sem):
    return pltpu.CompilerParams(dimension_semantics=sem,
                                vmem_limit_bytes=VMEM_LIMIT_BYTES)


def _const_spec(shape):
    nd = len(shape)
    return pl.BlockSpec(shape, lambda *_: (0,) * nd, pipeline_mode=pl.Buffered(1))


def _rms(x, g, eps):
    ms = jnp.mean(x * x, axis=-1, keepdims=True)
    return x * lax.rsqrt(ms + eps) * g


def _dot(a, b):
    return jnp.dot(a, b, preferred_element_type=F32)


def _ffn_body(*refs, d_ff, chunk, final):
    if final:
        x_ref, g_ref, win_ref, wout_ref, fg_ref, o_ref, acc_ref = refs
    else:
        x_ref, g_ref, win_ref, wout_ref, o_ref, acc_ref = refs
    x = x_ref[...]
    h = _rms(x, g_ref[...], NORM_EPS).astype(BF16)
    for c in range(d_ff // chunk):
        lo = c * chunk
        gate = _dot(h, win_ref[:, lo:lo + chunk])
        up = _dot(h, win_ref[:, d_ff + lo:d_ff + lo + chunk])
        act = (gate * jax.nn.sigmoid(gate) * up).astype(BF16)
        part = _dot(act, wout_ref[lo:lo + chunk, :])
        if c == 0:
            acc_ref[...] = part
        else:
            acc_ref[...] += part
    y = x + 0.5 * acc_ref[...]
    if final:
        y = _rms(y, fg_ref[...], NORM_EPS)
    o_ref[...] = y


def _ffn(x2, g, w_in, w_out, final_g=None, *, tm=512):
    n, d = x2.shape
    d_ff = w_out.shape[0]
    final = final_g is not None
    in_specs = [
        pl.BlockSpec((tm, d), lambda i: (i, 0)),
        _const_spec((1, d)),
        _const_spec(w_in.shape),
        _const_spec(w_out.shape),
    ]
    args = [x2, g.reshape(1, d), w_in, w_out]
    if final:
        in_specs.append(_const_spec((1, d)))
        args.append(final_g.reshape(1, d))
    return pl.pallas_call(
        functools.partial(_ffn_body, d_ff=d_ff, chunk=MXU_COLS, final=final),
        grid=(n // tm,),
        in_specs=in_specs,
        out_specs=pl.BlockSpec((tm, d), lambda i: (i, 0)),
        out_shape=jax.ShapeDtypeStruct((n, d), F32),
        scratch_shapes=[pltpu.VMEM((tm, d), F32)],
        compiler_params=_params("parallel"),
        name="ffn_final" if final else "ffn",
    )(*args)


def _inproj_body(x_ref, g_ref, w_ref, p_ref, q_ref, k_ref, v_ref, lx_ref, lg_ref):
    h = _rms(x_ref[...], g_ref[...], NORM_EPS).astype(BF16)
    qk = ATTN_HEADS * 2 * ATTN_QK_DIM
    off = 0

    def proj(width):
        nonlocal off
        out = _dot(h, w_ref[:, off:off + width])
        off += width
        return out

    p_ref[...] = proj(POOL_WIDTH)
    q_ref[...] = (proj(qk) * (ATTN_QK_DIM ** -0.5)).astype(BF16)
    k_ref[...] = proj(qk).astype(BF16)
    v_ref[...] = proj(ATTN_WIDTH).astype(BF16)
    lx_ref[...] = proj(LRU_WIDTH)
    lg_ref[...] = proj(LRU_WIDTH)


def _inproj(x2, g, w, *, tm=512):
    n, d = x2.shape
    qk = ATTN_HEADS * 2 * ATTN_QK_DIM
    widths = (POOL_WIDTH, qk, qk, ATTN_WIDTH, LRU_WIDTH, LRU_WIDTH)
    dtypes = (F32, BF16, BF16, BF16, F32, F32)
    return pl.pallas_call(
        _inproj_body,
        grid=(n // tm,),
        in_specs=[pl.BlockSpec((tm, d), lambda i: (i, 0)),
                  _const_spec((1, d)),
                  _const_spec(w.shape)],
        out_specs=[pl.BlockSpec((tm, wd), lambda i: (i, 0)) for wd in widths],
        out_shape=[jax.ShapeDtypeStruct((n, wd), dt) for wd, dt in zip(widths, dtypes)],
        compiler_params=_params("parallel"),
        name="inproj",
    )(x2, g.reshape(1, d), w)


POOL_PAD = 16


def _pool_body(p_ref, w_ref, sc_ref, o_ref, pad_ref, *, seq, rows):
    zeros = jnp.zeros((POOL_PAD, POOL_WIDTH), F32)
    pad_ref[0:POOL_PAD, :] = zeros
    pad_ref[seq + POOL_PAD:seq + 2 * POOL_PAD, :] = zeros
    pad_ref[POOL_PAD:seq + POOL_PAD, :] = p_ref[0]

    low = lax.broadcasted_iota(jnp.int32, (rows, LANES), 1) < LANES // 2
    row = lax.broadcasted_iota(jnp.int32, (rows, LANES), 0)

    def doubled(a, d):
        return a[d:, :] + a[:a.shape[0] - d, :]

    def chunk(c, carry):
        t0 = pl.multiple_of(c * rows, rows)
        xs = pad_ref[pl.ds(t0, rows + 2 * POOL_PAD), :]
        xa, xb = xs[:, :LANES], xs[:, LANES:]
        a2 = doubled(xa, 1)
        a4 = doubled(a2, 2)
        b8 = doubled(doubled(doubled(xb, 1), 2), 4)
        b16 = doubled(b8, 8)
        sum_a = jnp.where(low, a2[POOL_PAD - 1:POOL_PAD - 1 + rows], a4[POOL_PAD - 2:POOL_PAD - 2 + rows])
        sum_b = jnp.where(low, b8[POOL_PAD - 4:POOL_PAD - 4 + rows], b16[POOL_PAD - 8:POOL_PAD - 8 + rows])
        t = t0 + row

        def count(half):
            return (jnp.minimum(t + half, seq) - jnp.maximum(t - half, 0)).astype(F32)

        cnt_a = jnp.where(low, count(1), count(2))
        cnt_b = jnp.where(low, count(4), count(8))
        mix_a = sum_a / cnt_a - xa[POOL_PAD:POOL_PAD + rows]
        mix_b = sum_b / cnt_b - xb[POOL_PAD:POOL_PAD + rows]
        mixed = jnp.concatenate([mix_a, mix_b], axis=1).astype(BF16)
        y = _dot(mixed, w_ref[...]) * sc_ref[...]
        o_ref[0, pl.ds(t0, rows), :] = y.astype(BF16)
        return carry

    lax.fori_loop(0, seq // rows, chunk, 0)


def _pool(p3, w_bd, scale, *, rows=512):
    b, s, c = p3.shape
    return pl.pallas_call(
        functools.partial(_pool_body, seq=s, rows=rows),
        grid=(b,),
        in_specs=[pl.BlockSpec((1, s, c), lambda i: (i, 0, 0)),
                  _const_spec(w_bd.shape),
                  _const_spec((1, c))],
        out_specs=pl.BlockSpec((1, s, c), lambda i: (i, 0, 0)),
        out_shape=jax.ShapeDtypeStruct((b, s, c), BF16),
        scratch_shapes=[pltpu.VMEM((s + 2 * POOL_PAD, c), F32)],
        compiler_params=_params("parallel"),
        name="pool",
    )(p3, w_bd, scale.reshape(1, c))


LRU_PAD = 8
SCAN_ROWS = 8


def _softplus(x):
    return jnp.maximum(x, 0.0) + jnp.log1p(jnp.exp(-jnp.abs(x)))


def _lru_body(lx_ref, lg_ref, cw_ref, cb_ref, wg_ref, bg_ref, lam_ref, o_ref,
              pad_ref, af_ref, uf_ref, ab_ref, ub_ref, *, seq, rows):
    w = LRU_WIDTH
    zeros = jnp.zeros((LRU_PAD, w), F32)
    pad_ref[0:LRU_PAD, :] = zeros
    pad_ref[seq + LRU_PAD:seq + 2 * LRU_PAD, :] = zeros
    pad_ref[LRU_PAD:seq + LRU_PAD, :] = lx_ref[0]

    decay = -LRU_C * _softplus(-lam_ref[...])
    lead = LRU_CONV_WIDTH // 2

    def gates(c, carry):
        t0 = pl.multiple_of(c * rows, rows)
        xs = pad_ref[pl.ds(t0, rows + 2 * LRU_PAD), :]
        xf = cb_ref[...]
        for j in range(LRU_CONV_WIDTH):
            s0 = LRU_PAD - lead + j
            xf = xf + cw_ref[j:j + 1, :] * xs[s0:s0 + rows]
        g = _dot(xf.astype(BF16), wg_ref[...]) + bg_ref[...]
        for d, (a_ref, u_ref) in enumerate(((af_ref, uf_ref), (ab_ref, ub_ref))):
            r = jax.nn.sigmoid(g[:, (2 * d) * w:(2 * d + 1) * w])
            i = jax.nn.sigmoid(g[:, (2 * d + 1) * w:(2 * d + 2) * w])
            log_a = decay[d:d + 1, :] * r
            a = jnp.exp(log_a)
            a_ref[pl.ds(t0, rows), :] = a
            one_minus_a2 = -jnp.tanh(log_a) * (a * a + 1.0)
            u_ref[pl.ds(t0, rows), :] = jnp.sqrt(one_minus_a2) * (i * xf)
        return carry

    lax.fori_loop(0, seq // rows, gates, 0)

    def scan(i, carry):
        hf, hb = carry
        base_f = pl.multiple_of(i * SCAN_ROWS, SCAN_ROWS)
        base_b = pl.multiple_of(seq - SCAN_ROWS - i * SCAN_ROWS, SCAN_ROWS)
        for r in range(SCAN_ROWS):
            tf = base_f + r
            tb = base_b + (SCAN_ROWS - 1 - r)
            hf = af_ref[pl.ds(tf, 1), :] * hf + uf_ref[pl.ds(tf, 1), :]
            uf_ref[pl.ds(tf, 1), :] = hf
            hb = ab_ref[pl.ds(tb, 1), :] * hb + ub_ref[pl.ds(tb, 1), :]
            ub_ref[pl.ds(tb, 1), :] = hb
        return hf, hb

    h0 = jnp.zeros((1, w), F32)
    lax.fori_loop(0, seq // SCAN_ROWS, scan, (h0, h0))

    def gate_out(c, carry):
        t0 = pl.multiple_of(c * rows, rows)
        h = uf_ref[pl.ds(t0, rows), :] + ub_ref[pl.ds(t0, rows), :]
        y = jax.nn.gelu(lg_ref[0, pl.ds(t0, rows), :]) * h
        o_ref[0, pl.ds(t0, rows), :] = y.astype(BF16)
        return carry

    lax.fori_loop(0, seq // rows, gate_out, 0)


def _lru(lx3, lg3, conv_w, conv_b, w_gates, b_gates, lam, *, rows=512):
    b, s, c = lx3.shape
    seq_spec = pl.BlockSpec((1, s, c), lambda i: (i, 0, 0))
    return pl.pallas_call(
        functools.partial(_lru_body, seq=s, rows=rows),
        grid=(b,),
        in_specs=[seq_spec, seq_spec,
                  _const_spec(conv_w.shape), _const_spec((1, c)),
                  _const_spec(w_gates.shape), _const_spec((1, 4 * c)),
                  _const_spec(lam.shape)],
        out_specs=seq_spec,
        out_shape=jax.ShapeDtypeStruct((b, s, c), BF16),
        scratch_shapes=[pltpu.VMEM((s + 2 * LRU_PAD, c), F32)]
        + [pltpu.VMEM((s, c), F32) for _ in range(4)],
        compiler_params=_params("parallel"),
        name="lru",
    )(lx3, lg3, conv_w, conv_b.reshape(1, c), w_gates, b_gates.reshape(1, 4 * c), lam)


def _attn_body(slopes_ref, lam_ref, g_ref, q_ref, k_ref, v_ref, o_ref, acc1_ref, acc2_ref,
               *, tq, tk, seq, lam_init):
    head = pl.program_id(1)
    q0 = pl.program_id(2) * tq
    neg_slope = -slopes_ref[head]

    qf = q_ref[0].astype(F32)
    lane = lax.broadcasted_iota(jnp.int32, qf.shape, 1)
    q1 = jnp.where(lane < ATTN_QK_DIM, qf, 0.0).astype(BF16)
    q2 = jnp.where(lane >= ATTN_QK_DIM, qf, 0.0).astype(BF16)

    rel = (q0 + lax.broadcasted_iota(jnp.int32, (tq, tk), 0)
           - lax.broadcasted_iota(jnp.int32, (tq, tk), 1))

    acc1_ref[...] = jnp.zeros_like(acc1_ref)
    acc2_ref[...] = jnp.zeros_like(acc2_ref)

    nt = (((1,), (1,)), ((), ()))

    def step(j, carry):
        m1, l1, m2, l2 = carry
        k0 = pl.multiple_of(j * tk, tk)
        kc = k_ref[0, pl.ds(k0, tk), :]
        vc = v_ref[0, pl.ds(k0, tk), :]
        bias = jnp.abs(rel - k0).astype(F32) * neg_slope

        def one_map(qm, m, l, acc_ref):
            s = lax.dot_general(qm, kc, nt, preferred_element_type=F32) + bias
            m_new = jnp.maximum(m, jnp.max(s, axis=-1, keepdims=True))
            alpha = jnp.exp(m - m_new)
            p = jnp.exp(s - m_new)
            l_new = alpha * l + jnp.sum(p, axis=-1, keepdims=True)
            acc_ref[...] = alpha * acc_ref[...] + _dot(p.astype(BF16), vc)
            return m_new, l_new

        m1, l1 = one_map(q1, m1, l1, acc1_ref)
        m2, l2 = one_map(q2, m2, l2, acc2_ref)
        return m1, l1, m2, l2

    m0 = jnp.full((tq, 1), -jnp.inf, F32)
    z0 = jnp.zeros((tq, 1), F32)
    _, l1, _, l2 = lax.fori_loop(0, seq // tk, step, (m0, z0, m0, z0))

    lp = lam_ref[...]
    lam = (jnp.exp(jnp.sum(lp[0:1] * lp[1:2], axis=-1, keepdims=True))
           - jnp.exp(jnp.sum(lp[2:3] * lp[3:4], axis=-1, keepdims=True)) + lam_init)
    o = acc1_ref[...] / l1 - lam * (acc2_ref[...] / l2)
    o = o * lax.rsqrt(jnp.mean(o * o, axis=-1, keepdims=True) + SUBLN_EPS) * g_ref[...]
    o_ref[0] = (o * (1.0 - lam_init)).astype(BF16)


def _attn(q3, k3, v3, lam_params, subln_g, lam_init, *, tq=256, tk=512):
    b, s, _ = q3.shape
    hd = ATTN_V_DIM
    slopes = jnp.asarray([2.0 ** (-8.0 / ATTN_HEADS * (i + 1)) for i in range(ATTN_HEADS)], F32)
    kv_spec = pl.BlockSpec((1, s, hd), lambda bi, hi, qi: (bi, 0, hi))
    q_spec = pl.BlockSpec((1, tq, hd), lambda bi, hi, qi: (bi, qi, hi))
    return pl.pallas_call(
        functools.partial(_attn_body, tq=tq, tk=tk, seq=s, lam_init=lam_init),
        grid=(b, ATTN_HEADS, s // tq),
        in_specs=[pl.BlockSpec(memory_space=pltpu.SMEM),
                  _const_spec(lam_params.shape),
                  _const_spec((1, hd)),
                  q_spec, kv_spec, kv_spec],
        out_specs=q_spec,
        out_shape=jax.ShapeDtypeStruct((b, s, ATTN_WIDTH), BF16),
        scratch_shapes=[pltpu.VMEM((tq, hd), F32), pltpu.VMEM((tq, hd), F32)],
        compiler_params=_params("parallel", "parallel", "parallel"),
        name="diff_attn",
    )(slopes, lam_params, subln_g.reshape(1, hd), q3, k3, v3)


def _merge_body(x_ref, g_ref, wgate_ref, mb_ref, pool_ref, attn_ref, lru_ref,
                wbp_ref, wba_ref, wbl_ref, wout_ref, o_ref):
    x = x_ref[...]
    d = x.shape[-1]
    h = _rms(x, g_ref[...], NORM_EPS).astype(BF16)
    merged = None
    for i, (br_ref, wb_ref) in enumerate(((pool_ref, wbp_ref), (attn_ref, wba_ref), (lru_ref, wbl_ref))):
        logits = _dot(h, wgate_ref[:, i * d:(i + 1) * d]) + mb_ref[i:i + 1, :]
        term = jax.nn.sigmoid(logits) * _dot(br_ref[...], wb_ref[...])
        merged = term if merged is None else merged + term
    o_ref[...] = x + _dot(merged.astype(BF16), wout_ref[...])


def _merge(x2, g, w_gate, merge_bias, pool2, attn2, lru2, wbp, wba, wbl, w_out, *, tm=512):
    n, d = x2.shape

    def tok(width):
        return pl.BlockSpec((tm, width), lambda i: (i, 0))

    return pl.pallas_call(
        _merge_body,
        grid=(n // tm,),
        in_specs=[tok(d), _const_spec((1, d)), _const_spec(w_gate.shape), _const_spec(merge_bias.shape),
                  tok(pool2.shape[1]), tok(attn2.shape[1]), tok(lru2.shape[1]),
                  _const_spec(wbp.shape), _const_spec(wba.shape), _const_spec(wbl.shape),
                  _const_spec(w_out.shape)],
        out_specs=tok(d),
        out_shape=jax.ShapeDtypeStruct((n, d), F32),
        compiler_params=_params("parallel"),
        name="merge",
    )(x2, g.reshape(1, d), w_gate, merge_bias, pool2, attn2, lru2, wbp, wba, wbl, w_out)


def _block_diag(w):
    g, c, _ = w.shape
    eye = jnp.eye(g, dtype=w.dtype)
    return (eye[:, None, :, None] * w[:, :, None, :]).reshape(g * c, g * c)


def kernel(x, ffn1_norm, ffn1_w_in, ffn1_w_out, mix_norm, w_in, pool_w, pool_scale, attn_lambda, attn_subln, lru_conv_w, lru_conv_b, lru_w_a, lru_b_a, lru_w_x, lru_b_x, lru_lambda, w_branch_pool, w_branch_attn, w_branch_lru, merge_bias, w_out, ffn2_norm, ffn2_w_in, ffn2_w_out, final_norm):
    b, s, d = x.shape
    depth = ffn1_norm.shape[0]
    qk = ATTN_HEADS * 2 * ATTN_QK_DIM
    mix_cols = POOL_WIDTH + 2 * qk + ATTN_WIDTH + 2 * LRU_WIDTH
    x2 = x.reshape(b * s, d)
    for l in range(depth):
        lam_init = 0.8 - 0.6 * math.exp(-0.3 * l)
        last = l == depth - 1
        x2 = _ffn(x2, ffn1_norm[l], ffn1_w_in[l].astype(BF16), ffn1_w_out[l].astype(BF16))

        p, q, k, v, lx, lg = _inproj(x2, mix_norm[l], w_in[l, :, :mix_cols].astype(BF16))
        pool = _pool(p.reshape(b, s, -1), _block_diag(pool_w[l]).astype(BF16), pool_scale[l])
        w_gates = jnp.concatenate(
            [_block_diag(lru_w_a[l, 0]), _block_diag(lru_w_x[l, 0]),
             _block_diag(lru_w_a[l, 1]), _block_diag(lru_w_x[l, 1])], axis=1).astype(BF16)
        b_gates = jnp.concatenate([lru_b_a[l, 0], lru_b_x[l, 0], lru_b_a[l, 1], lru_b_x[l, 1]])
        lru = _lru(lx.reshape(b, s, -1), lg.reshape(b, s, -1), lru_conv_w[l], lru_conv_b[l],
                   w_gates, b_gates, lru_lambda[l])
        attn = _attn(q.reshape(b, s, -1), k.reshape(b, s, -1), v.reshape(b, s, -1),
                     attn_lambda[l], attn_subln[l], lam_init)
        x2 = _merge(x2, mix_norm[l], w_in[l, :, mix_cols:].astype(BF16), merge_bias[l],
                    pool.reshape(b * s, -1), attn.reshape(b * s, -1), lru.reshape(b * s, -1),
                    w_branch_pool[l].astype(BF16), w_branch_attn[l].astype(BF16),
                    w_branch_lru[l].astype(BF16), w_out[l].astype(BF16))

        x2 = _ffn(x2, ffn2_norm[l], ffn2_w_in[l].astype(BF16), ffn2_w_out[l].astype(BF16),
                  final_norm if last else None)
    return x2.reshape(b, s, d)
```

```python
import functools
import math

import jax
import jax.numpy as jnp
from jax import lax
from jax.experimental import pallas as pl
from jax.experimental.pallas import tpu as pltpu

F32 = jnp.float32
BF16 = jnp.bfloat16

NORM_EPS = 1e-6
SUBLN_EPS = 1e-5
LRU_C = 8.0

POOL_WINDOWS = (2, 4, 8, 16)
POOL_WIDTH = 256
ATTN_HEADS = 4
ATTN_QK_DIM = 64
ATTN_V_DIM = 128
ATTN_WIDTH = ATTN_HEADS * ATTN_V_DIM
LRU_WIDTH = 256
LRU_CONV_WIDTH = 4

LOG2E = math.log2(math.e)
Q_SCALE = ATTN_QK_DIM ** -0.5 * LOG2E

LANES = 128
MXU_COLS = 256
VMEM_LIMIT_BYTES = 52 * 1024 * 1024


def _params(*sem):
    return pltpu.CompilerParams(dimension_semantics=sem,
                                vmem_limit_bytes=VMEM_LIMIT_BYTES)


def _const_spec(shape):
    nd = len(shape)
    return pl.BlockSpec(shape, lambda *_: (0,) * nd, pipeline_mode=pl.Buffered(1))


def _rms(x, g, eps):
    ms = jnp.mean(x * x, axis=-1, keepdims=True)
    return x * lax.rsqrt(ms + eps) * g


def _dot(a, b):
    return jnp.dot(a, b, preferred_element_type=F32)


def _ffn_body(*refs, d_ff, chunk, final):
    if final:
        x_ref, g_ref, win_ref, wout_ref, fg_ref, o_ref, acc_ref = refs
    else:
        x_ref, g_ref, win_ref, wout_ref, o_ref, acc_ref = refs
    x = x_ref[...]
    h = _rms(x, g_ref[...], NORM_EPS).astype(BF16)
    for c in range(d_ff // chunk):
        lo = c * chunk
        gate = _dot(h, win_ref[:, lo:lo + chunk])
        up = _dot(h, win_ref[:, d_ff + lo:d_ff + lo + chunk])
        act = (gate * jax.nn.sigmoid(gate) * up).astype(BF16)
        part = _dot(act, wout_ref[lo:lo + chunk, :])
        if c == 0:
            acc_ref[...] = part
        else:
            acc_ref[...] += part
    y = x + 0.5 * acc_ref[...]
    if final:
        y = _rms(y, fg_ref[...], NORM_EPS)
    o_ref[...] = y


def _ffn(x2, g, w_in, w_out, final_g=None, *, tm=512):
    n, d = x2.shape
    d_ff = w_out.shape[0]
    final = final_g is not None
    in_specs = [
        pl.BlockSpec((tm, d), lambda i: (i, 0)),
        _const_spec((1, d)),
        _const_spec(w_in.shape),
        _const_spec(w_out.shape),
    ]
    args = [x2, g.reshape(1, d), w_in, w_out]
    if final:
        in_specs.append(_const_spec((1, d)))
        args.append(final_g.reshape(1, d))
    return pl.pallas_call(
        functools.partial(_ffn_body, d_ff=d_ff, chunk=MXU_COLS, final=final),
        grid=(n // tm,),
        in_specs=in_specs,
        out_specs=pl.BlockSpec((tm, d), lambda i: (i, 0)),
        out_shape=jax.ShapeDtypeStruct((n, d), F32),
        scratch_shapes=[pltpu.VMEM((tm, d), F32)],
        compiler_params=_params("parallel"),
        name="ffn_final" if final else "ffn",
    )(*args)


def _inproj_body(x_ref, g_ref, w_ref, p_ref, q_ref, k_ref, v_ref, lx_ref, lg_ref):
    h = _rms(x_ref[...], g_ref[...], NORM_EPS).astype(BF16)
    qk = ATTN_HEADS * 2 * ATTN_QK_DIM
    off = 0

    def proj(width):
        nonlocal off
        out = _dot(h, w_ref[:, off:off + width])
        off += width
        return out

    p_ref[...] = proj(POOL_WIDTH)
    q_ref[...] = (proj(qk) * Q_SCALE).astype(BF16)
    k_ref[...] = proj(qk).astype(BF16)
    v_ref[...] = proj(ATTN_WIDTH).astype(BF16)
    lx_ref[...] = proj(LRU_WIDTH)
    lg_ref[...] = proj(LRU_WIDTH)


def _inproj(x2, g, w, *, tm=512):
    n, d = x2.shape
    qk = ATTN_HEADS * 2 * ATTN_QK_DIM
    widths = (POOL_WIDTH, qk, qk, ATTN_WIDTH, LRU_WIDTH, LRU_WIDTH)
    dtypes = (F32, BF16, BF16, BF16, F32, F32)
    return pl.pallas_call(
        _inproj_body,
        grid=(n // tm,),
        in_specs=[pl.BlockSpec((tm, d), lambda i: (i, 0)),
                  _const_spec((1, d)),
                  _const_spec(w.shape)],
        out_specs=[pl.BlockSpec((tm, wd), lambda i: (i, 0)) for wd in widths],
        out_shape=[jax.ShapeDtypeStruct((n, wd), dt) for wd, dt in zip(widths, dtypes)],
        compiler_params=_params("parallel"),
        name="inproj",
    )(x2, g.reshape(1, d), w)


POOL_PAD = 16


def _pool_body(p_ref, w_ref, sc_ref, o_ref, pad_ref, *, seq, rows):
    zeros = jnp.zeros((POOL_PAD, POOL_WIDTH), F32)
    pad_ref[0:POOL_PAD, :] = zeros
    pad_ref[seq + POOL_PAD:seq + 2 * POOL_PAD, :] = zeros
    pad_ref[POOL_PAD:seq + POOL_PAD, :] = p_ref[0]

    low = lax.broadcasted_iota(jnp.int32, (rows, LANES), 1) < LANES // 2
    row = lax.broadcasted_iota(jnp.int32, (rows, LANES), 0)

    def doubled(a, d):
        return a[d:, :] + a[:a.shape[0] - d, :]

    def chunk(c, carry):
        t0 = pl.multiple_of(c * rows, rows)
        xs = pad_ref[pl.ds(t0, rows + 2 * POOL_PAD), :]
        xa, xb = xs[:, :LANES], xs[:, LANES:]
        a2 = doubled(xa, 1)
        a4 = doubled(a2, 2)
        b8 = doubled(doubled(doubled(xb, 1), 2), 4)
        b16 = doubled(b8, 8)
        sum_a = jnp.where(low, a2[POOL_PAD - 1:POOL_PAD - 1 + rows], a4[POOL_PAD - 2:POOL_PAD - 2 + rows])
        sum_b = jnp.where(low, b8[POOL_PAD - 4:POOL_PAD - 4 + rows], b16[POOL_PAD - 8:POOL_PAD - 8 + rows])
        t = t0 + row

        def count(half):
            return (jnp.minimum(t + half, seq) - jnp.maximum(t - half, 0)).astype(F32)

        cnt_a = jnp.where(low, count(1), count(2))
        cnt_b = jnp.where(low, count(4), count(8))
        mix_a = sum_a / cnt_a - xa[POOL_PAD:POOL_PAD + rows]
        mix_b = sum_b / cnt_b - xb[POOL_PAD:POOL_PAD + rows]
        mixed = jnp.concatenate([mix_a, mix_b], axis=1).astype(BF16)
        y = _dot(mixed, w_ref[...]) * sc_ref[...]
        o_ref[0, pl.ds(t0, rows), :] = y.astype(BF16)
        return carry

    lax.fori_loop(0, seq // rows, chunk, 0)


def _pool(p3, w_bd, scale, *, rows=512):
    b, s, c = p3.shape
    return pl.pallas_call(
        functools.partial(_pool_body, seq=s, rows=rows),
        grid=(b,),
        in_specs=[pl.BlockSpec((1, s, c), lambda i: (i, 0, 0)),
                  _const_spec(w_bd.shape),
                  _const_spec((1, c))],
        out_specs=pl.BlockSpec((1, s, c), lambda i: (i, 0, 0)),
        out_shape=jax.ShapeDtypeStruct((b, s, c), BF16),
        scratch_shapes=[pltpu.VMEM((s + 2 * POOL_PAD, c), F32)],
        compiler_params=_params("parallel"),
        name="pool",
    )(p3, w_bd, scale.reshape(1, c))


LRU_PAD = 8
SCAN_ROWS = 8


def _softplus(x):
    return jnp.maximum(x, 0.0) + jnp.log1p(jnp.exp(-jnp.abs(x)))


def _lru_body(lx_ref, lg_ref, cw_ref, cb_ref, wg_ref, bg_ref, lam_ref, o_ref,
              pad_ref, af_ref, uf_ref, ab_ref, ub_ref, *, seq, rows):
    w = LRU_WIDTH
    zeros = jnp.zeros((LRU_PAD, w), F32)
    pad_ref[0:LRU_PAD, :] = zeros
    pad_ref[seq + LRU_PAD:seq + 2 * LRU_PAD, :] = zeros
    pad_ref[LRU_PAD:seq + LRU_PAD, :] = lx_ref[0]

    decay = -LRU_C * _softplus(-lam_ref[...])
    lead = LRU_CONV_WIDTH // 2

    def gates(c, carry):
        t0 = pl.multiple_of(c * rows, rows)
        xs = pad_ref[pl.ds(t0, rows + 2 * LRU_PAD), :]
        xf = cb_ref[...]
        for j in range(LRU_CONV_WIDTH):
            s0 = LRU_PAD - lead + j
            xf = xf + cw_ref[j:j + 1, :] * xs[s0:s0 + rows]
        g = _dot(xf.astype(BF16), wg_ref[...]) + bg_ref[...]
        for d, (a_ref, u_ref) in enumerate(((af_ref, uf_ref), (ab_ref, ub_ref))):
            r = jax.nn.sigmoid(g[:, (2 * d) * w:(2 * d + 1) * w])
            i = jax.nn.sigmoid(g[:, (2 * d + 1) * w:(2 * d + 2) * w])
            log_a = decay[d:d + 1, :] * r
            a = jnp.exp(log_a)
            a_ref[pl.ds(t0, rows), :] = a
            one_minus_a2 = -jnp.tanh(log_a) * (a * a + 1.0)
            u_ref[pl.ds(t0, rows), :] = jnp.sqrt(one_minus_a2) * (i * xf)
        return carry

    lax.fori_loop(0, seq // rows, gates, 0)

    def scan(i, carry):
        hf, hb = carry
        base_f = pl.multiple_of(i * SCAN_ROWS, SCAN_ROWS)
        base_b = pl.multiple_of(seq - SCAN_ROWS - i * SCAN_ROWS, SCAN_ROWS)
        for r in range(SCAN_ROWS):
            tf = base_f + r
            tb = base_b + (SCAN_ROWS - 1 - r)
            hf = af_ref[pl.ds(tf, 1), :] * hf + uf_ref[pl.ds(tf, 1), :]
            uf_ref[pl.ds(tf, 1), :] = hf
            hb = ab_ref[pl.ds(tb, 1), :] * hb + ub_ref[pl.ds(tb, 1), :]
            ub_ref[pl.ds(tb, 1), :] = hb
        return hf, hb

    h0 = jnp.zeros((1, w), F32)
    lax.fori_loop(0, seq // SCAN_ROWS, scan, (h0, h0))

    def gate_out(c, carry):
        t0 = pl.multiple_of(c * rows, rows)
        h = uf_ref[pl.ds(t0, rows), :] + ub_ref[pl.ds(t0, rows), :]
        y = jax.nn.gelu(lg_ref[0, pl.ds(t0, rows), :]) * h
        o_ref[0, pl.ds(t0, rows), :] = y.astype(BF16)
        return carry

    lax.fori_loop(0, seq // rows, gate_out, 0)


def _lru(lx3, lg3, conv_w, conv_b, w_gates, b_gates, lam, *, rows=512):
    b, s, c = lx3.shape
    seq_spec = pl.BlockSpec((1, s, c), lambda i: (i, 0, 0))
    return pl.pallas_call(
        functools.partial(_lru_body, seq=s, rows=rows),
        grid=(b,),
        in_specs=[seq_spec, seq_spec,
                  _const_spec(conv_w.shape), _const_spec((1, c)),
                  _const_spec(w_gates.shape), _const_spec((1, 4 * c)),
                  _const_spec(lam.shape)],
        out_specs=seq_spec,
        out_shape=jax.ShapeDtypeStruct((b, s, c), BF16),
        scratch_shapes=[pltpu.VMEM((s + 2 * LRU_PAD, c), F32)]
        + [pltpu.VMEM((s, c), F32) for _ in range(4)],
        compiler_params=_params("parallel"),
        name="lru",
    )(lx3, lg3, conv_w, conv_b.reshape(1, c), w_gates, b_gates.reshape(1, 4 * c), lam)


ALIBI_SPLIT = 64


def _bf16_split3(c):
    pieces = []
    for _ in range(3):
        p = c.astype(BF16).astype(F32)
        pieces.append(p)
        c = c - p
    return pieces


def _alibi_tables(seq):
    assert seq <= ALIBI_SPLIT * 256
    slopes = jnp.asarray([2.0 ** (-8.0 / ATTN_HEADS * (i + 1)) for i in range(ATTN_HEADS)], F32)
    coef = slopes * LOG2E
    pos = jnp.arange(seq)
    parts = [jnp.broadcast_to((ALIBI_SPLIT * (pos // ALIBI_SPLIT)).astype(F32), (ATTN_HEADS, seq)),
             jnp.broadcast_to((pos % ALIBI_SPLIT).astype(F32), (ATTN_HEADS, seq))]
    cs = [jnp.broadcast_to(c[:, None], (ATTN_HEADS, seq)) for c in _bf16_split3(coef)]
    q_cols = [part for part in parts for _ in cs] + [c for _ in parts for c in cs]
    k_cols = [-c for _ in parts for c in cs] + [part for part in parts for _ in cs]
    pad = ((0, 0), (0, 0), (0, LANES - len(q_cols)))
    q_tab = jnp.pad(jnp.stack(q_cols, axis=-1), pad).astype(BF16)
    k_tab = jnp.pad(jnp.stack(k_cols, axis=-1), pad).astype(BF16)
    return coef, q_tab, k_tab


def _attn_body(coef_ref, lam_ref, g_ref, q_ref, qtab_ref, k_ref, ktab_ref, v_ref, o_ref,
               acc1_ref, acc2_ref, qa_ref, *, tq, tk, seq, lam_init):
    head = pl.program_id(1)
    q0 = pl.program_id(2) * tq
    diag = q0 // tk

    qf = q_ref[0].astype(F32)
    lane = lax.broadcasted_iota(jnp.int32, qf.shape, 1)
    q1 = jnp.where(lane < ATTN_QK_DIM, qf, 0.0).astype(BF16)
    q2 = jnp.where(lane >= ATTN_QK_DIM, qf, 0.0).astype(BF16)
    tab_before = qtab_ref[0]
    tab_after = (-tab_before.astype(F32)).astype(BF16)
    for side, tab in enumerate((tab_before, tab_after)):
        qa_ref[side, 0] = jnp.concatenate([q1, tab], axis=1)
        qa_ref[side, 1] = jnp.concatenate([q2, tab], axis=1)

    acc1_ref[...] = jnp.zeros_like(acc1_ref)
    acc2_ref[...] = jnp.zeros_like(acc2_ref)

    nt = (((1,), (1,)), ((), ()))
    ones = jnp.ones((tk, LANES), BF16)

    def update(s, m, acc_ref, k0):
        vc = jnp.concatenate([v_ref[0, pl.ds(k0, tk), :], ones], axis=1)
        m_new = jnp.maximum(m, jnp.max(s, axis=-1, keepdims=True))
        p = jnp.exp2(s - m_new)
        acc_ref[...] = jnp.exp2(m - m_new) * acc_ref[...] + _dot(p.astype(BF16), vc)
        return m_new

    def folded_scores(jj):
        j = jj + (jj >= diag).astype(jnp.int32)
        side = (j > diag).astype(jnp.int32)
        k0 = pl.multiple_of(j * tk, tk)
        ka = jnp.concatenate([k_ref[0, pl.ds(k0, tk), :], ktab_ref[0, pl.ds(k0, tk), :]], axis=1)
        s1 = lax.dot_general(qa_ref[side, 0], ka, nt, preferred_element_type=F32)
        s2 = lax.dot_general(qa_ref[side, 1], ka, nt, preferred_element_type=F32)
        return s1, s2, k0

    def diag_scores():
        k0 = pl.multiple_of(diag * tk, tk)
        kc = k_ref[0, pl.ds(k0, tk), :]
        rel = (q0 - k0 + lax.broadcasted_iota(jnp.int32, (tq, tk), 0)
               - lax.broadcasted_iota(jnp.int32, (tq, tk), 1))
        bias = jnp.abs(rel).astype(F32) * (-coef_ref[head])
        s1 = lax.dot_general(q1, kc, nt, preferred_element_type=F32) + bias
        s2 = lax.dot_general(q2, kc, nt, preferred_element_type=F32) + bias
        return s1, s2, k0

    m1 = m2 = jnp.full((tq, 1), -jnp.inf, F32)
    cur = diag_scores()
    for jj in range(seq // tk - 1):
        nxt = folded_scores(jnp.int32(jj))
        m1 = update(cur[0], m1, acc1_ref, cur[2])
        m2 = update(cur[1], m2, acc2_ref, cur[2])
        cur = nxt
    update(cur[0], m1, acc1_ref, cur[2])
    update(cur[1], m2, acc2_ref, cur[2])

    lp = lam_ref[...]
    lam = (jnp.exp(jnp.sum(lp[0:1] * lp[1:2], axis=-1, keepdims=True))
           - jnp.exp(jnp.sum(lp[2:3] * lp[3:4], axis=-1, keepdims=True)) + lam_init)
    hd = ATTN_V_DIM
    o = (acc1_ref[:, :hd] / acc1_ref[:, hd:]) - lam * (acc2_ref[:, :hd] / acc2_ref[:, hd:])
    o = o * lax.rsqrt(jnp.mean(o * o, axis=-1, keepdims=True) + SUBLN_EPS) * g_ref[...]
    o_ref[0] = (o * (1.0 - lam_init)).astype(BF16)


def _attn(q3, k3, v3, lam_params, subln_g, lam_init, *, tq=512, tk=512):
    b, s, _ = q3.shape
    hd = ATTN_V_DIM
    assert tk % tq == 0 and s % tk == 0
    coef, q_tab, k_tab = _alibi_tables(s)
    kv_spec = pl.BlockSpec((1, s, hd), lambda bi, hi, qi: (bi, 0, hi))
    q_spec = pl.BlockSpec((1, tq, hd), lambda bi, hi, qi: (bi, qi, hi))
    return pl.pallas_call(
        functools.partial(_attn_body, tq=tq, tk=tk, seq=s, lam_init=lam_init),
        grid=(b, ATTN_HEADS, s // tq),
        in_specs=[pl.BlockSpec(memory_space=pltpu.SMEM),
                  _const_spec(lam_params.shape),
                  _const_spec((1, hd)),
                  q_spec,
                  pl.BlockSpec((1, tq, LANES), lambda bi, hi, qi: (hi, qi, 0)),
                  kv_spec,
                  pl.BlockSpec((1, s, LANES), lambda bi, hi, qi: (hi, 0, 0)),
                  kv_spec],
        out_specs=q_spec,
        out_shape=jax.ShapeDtypeStruct((b, s, ATTN_WIDTH), BF16),
        scratch_shapes=[pltpu.VMEM((tq, hd + LANES), F32), pltpu.VMEM((tq, hd + LANES), F32),
                        pltpu.VMEM((2, 2, tq, 2 * LANES), BF16)],
        compiler_params=_params("parallel", "parallel", "parallel"),
        name="diff_attn",
    )(coef, lam_params, subln_g.reshape(1, hd), q3, q_tab, k3, k_tab, v3)


def _merge_body(x_ref, g_ref, wgate_ref, mb_ref, pool_ref, attn_ref, lru_ref,
                wbp_ref, wba_ref, wbl_ref, wout_ref, o_ref):
    x = x_ref[...]
    d = x.shape[-1]
    h = _rms(x, g_ref[...], NORM_EPS).astype(BF16)
    merged = None
    for i, (br_ref, wb_ref) in enumerate(((pool_ref, wbp_ref), (attn_ref, wba_ref), (lru_ref, wbl_ref))):
        logits = _dot(h, wgate_ref[:, i * d:(i + 1) * d]) + mb_ref[i:i + 1, :]
        term = jax.nn.sigmoid(logits) * _dot(br_ref[...], wb_ref[...])
        merged = term if merged is None else merged + term
    o_ref[...] = x + _dot(merged.astype(BF16), wout_ref[...])


def _merge(x2, g, w_gate, merge_bias, pool2, attn2, lru2, wbp, wba, wbl, w_out, *, tm=512):
    n, d = x2.shape

    def tok(width):
        return pl.BlockSpec((tm, width), lambda i: (i, 0))

    return pl.pallas_call(
        _merge_body,
        grid=(n // tm,),
        in_specs=[tok(d), _const_spec((1, d)), _const_spec(w_gate.shape), _const_spec(merge_bias.shape),
                  tok(pool2.shape[1]), tok(attn2.shape[1]), tok(lru2.shape[1]),
                  _const_spec(wbp.shape), _const_spec(wba.shape), _const_spec(wbl.shape),
                  _const_spec(w_out.shape)],
        out_specs=tok(d),
        out_shape=jax.ShapeDtypeStruct((n, d), F32),
        compiler_params=_params("parallel"),
        name="merge",
    )(x2, g.reshape(1, d), w_gate, merge_bias, pool2, attn2, lru2, wbp, wba, wbl, w_out)


def _block_diag(w):
    g, c, _ = w.shape
    eye = jnp.eye(g, dtype=w.dtype)
    return (eye[:, None, :, None] * w[:, :, None, :]).reshape(g * c, g * c)


def kernel(x, ffn1_norm, ffn1_w_in, ffn1_w_out, mix_norm, w_in, pool_w, pool_scale, attn_lambda, attn_subln, lru_conv_w, lru_conv_b, lru_w_a, lru_b_a, lru_w_x, lru_b_x, lru_lambda, w_branch_pool, w_branch_attn, w_branch_lru, merge_bias, w_out, ffn2_norm, ffn2_w_in, ffn2_w_out, final_norm):
    b, s, d = x.shape
    depth = ffn1_norm.shape[0]
    qk = ATTN_HEADS * 2 * ATTN_QK_DIM
    mix_cols = POOL_WIDTH + 2 * qk + ATTN_WIDTH + 2 * LRU_WIDTH
    x2 = x.reshape(b * s, d)
    for l in range(depth):
        lam_init = 0.8 - 0.6 * math.exp(-0.3 * l)
        last = l == depth - 1
        x2 = _ffn(x2, ffn1_norm[l], ffn1_w_in[l].astype(BF16), ffn1_w_out[l].astype(BF16))

        p, q, k, v, lx, lg = _inproj(x2, mix_norm[l], w_in[l, :, :mix_cols].astype(BF16))
        pool = _pool(p.reshape(b, s, -1), _block_diag(pool_w[l]).astype(BF16), pool_scale[l])
        w_gates = jnp.concatenate(
            [_block_diag(lru_w_a[l, 0]), _block_diag(lru_w_x[l, 0]),
             _block_diag(lru_w_a[l, 1]), _block_diag(lru_w_x[l, 1])], axis=1).astype(BF16)
        b_gates = jnp.concatenate([lru_b_a[l, 0], lru_b_x[l, 0], lru_b_a[l, 1], lru_b_x[l, 1]])
        lru = _lru(lx.reshape(b, s, -1), lg.reshape(b, s, -1), lru_conv_w[l], lru_conv_b[l],
                   w_gates, b_gates, lru_lambda[l])
        attn = _attn(q.reshape(b, s, -1), k.reshape(b, s, -1), v.reshape(b, s, -1),
                     attn_lambda[l], attn_subln[l], lam_init)
        x2 = _merge(x2, mix_norm[l], w_in[l, :, mix_cols:].astype(BF16), merge_bias[l],
                    pool.reshape(b * s, -1), attn.reshape(b * s, -1), lru.reshape(b * s, -1),
                    w_branch_pool[l].astype(BF16), w_branch_attn[l].astype(BF16),
                    w_branch_lru[l].astype(BF16), w_out[l].astype(BF16))

        x2 = _ffn(x2, ffn2_norm[l], ffn2_w_in[l].astype(BF16), ffn2_w_out[l].astype(BF16),
                  final_norm if last else None)
    return x2.reshape(b, s, d)
```

```python
import functools
import math

import jax
import jax.numpy as jnp
from jax import lax
from jax.experimental import pallas as pl
from jax.experimental.pallas import tpu as pltpu

F32 = jnp.float32
BF16 = jnp.bfloat16

NORM_EPS = 1e-6
SUBLN_EPS = 1e-5
LRU_C = 8.0

POOL_WINDOWS = (2, 4, 8, 16)
POOL_WIDTH = 256
ATTN_HEADS = 4
ATTN_QK_DIM = 64
ATTN_V_DIM = 128
ATTN_WIDTH = ATTN_HEADS * ATTN_V_DIM
LRU_WIDTH = 256
LRU_CONV_WIDTH = 4

LOG2E = math.log2(math.e)
Q_SCALE = ATTN_QK_DIM ** -0.5 * LOG2E

LANES = 128
MXU_COLS = 256
VMEM_LIMIT_BYTES = 52 * 1024 * 1024


def _params(*sem):
    return pltpu.CompilerParams(dimension_semantics=sem,
                                vmem_limit_bytes=VMEM_LIMIT_BYTES)


def _const_spec(shape):
    nd = len(shape)
    return pl.BlockSpec(shape, lambda *_: (0,) * nd, pipeline_mode=pl.Buffered(1))


def _rms(x, g, eps):
    ms = jnp.mean(x * x, axis=-1, keepdims=True)
    return x * lax.rsqrt(ms + eps) * g


def _dot(a, b):
    return jnp.dot(a, b, preferred_element_type=F32)


def _dot_nt(a, b):
    return lax.dot_general(a, b, (((1,), (1,)), ((), ())), preferred_element_type=F32)


def _ffn_body(*refs, d_ff, chunk, final):
    if final:
        x_ref, g_ref, win_ref, wout_ref, fg_ref, o_ref, acc_ref = refs
    else:
        x_ref, g_ref, win_ref, wout_ref, o_ref, acc_ref = refs
    x = x_ref[...]
    h = _rms(x, g_ref[...], NORM_EPS).astype(BF16)
    for c in range(d_ff // chunk):
        lo = c * chunk
        gate = _dot(h, win_ref[:, lo:lo + chunk])
        up = _dot(h, win_ref[:, d_ff + lo:d_ff + lo + chunk])
        act = (gate * jax.nn.sigmoid(gate) * up).astype(BF16)
        part = _dot(act, wout_ref[lo:lo + chunk, :])
        if c == 0:
            acc_ref[...] = part
        else:
            acc_ref[...] += part
    y = x + 0.5 * acc_ref[...]
    if final:
        y = _rms(y, fg_ref[...], NORM_EPS)
    o_ref[...] = y


def _ffn(x2, g, w_in, w_out, final_g=None, *, tm=512):
    n, d = x2.shape
    d_ff = w_out.shape[0]
    final = final_g is not None
    in_specs = [
        pl.BlockSpec((tm, d), lambda i: (i, 0)),
        _const_spec((1, d)),
        _const_spec(w_in.shape),
        _const_spec(w_out.shape),
    ]
    args = [x2, g.reshape(1, d), w_in, w_out]
    if final:
        in_specs.append(_const_spec((1, d)))
        args.append(final_g.reshape(1, d))
    return pl.pallas_call(
        functools.partial(_ffn_body, d_ff=d_ff, chunk=MXU_COLS, final=final),
        grid=(n // tm,),
        in_specs=in_specs,
        out_specs=pl.BlockSpec((tm, d), lambda i: (i, 0)),
        out_shape=jax.ShapeDtypeStruct((n, d), F32),
        scratch_shapes=[pltpu.VMEM((tm, d), F32)],
        compiler_params=_params("parallel"),
        name="ffn_final" if final else "ffn",
    )(*args)


def _inproj_body(x_ref, g_ref, w_ref, p_ref, q_ref, k_ref, vt_ref, lx_ref, lg_ref):
    h = _rms(x_ref[...], g_ref[...], NORM_EPS).astype(BF16)
    qk = ATTN_HEADS * 2 * ATTN_QK_DIM
    off = 0

    def proj(width):
        nonlocal off
        out = _dot(h, w_ref[:, off:off + width])
        off += width
        return out

    p_ref[...] = proj(POOL_WIDTH)
    q_ref[...] = (proj(qk) * Q_SCALE).astype(BF16)
    k_ref[...] = proj(qk).astype(BF16)
    vt_ref[0] = proj(ATTN_WIDTH).T.astype(BF16)
    lx_ref[...] = proj(LRU_WIDTH)
    lg_ref[...] = proj(LRU_WIDTH)


def _inproj(x2, g, w, seq, *, tm=512):
    n, d = x2.shape
    assert seq % tm == 0
    tiles = seq // tm
    qk = ATTN_HEADS * 2 * ATTN_QK_DIM

    def tok(width, dtype):
        return pl.BlockSpec((tm, width), lambda i: (i, 0)), jax.ShapeDtypeStruct((n, width), dtype)

    vt = (pl.BlockSpec((1, ATTN_WIDTH, tm), lambda i: (i // tiles, 0, i % tiles)),
          jax.ShapeDtypeStruct((n // seq, ATTN_WIDTH, seq), BF16))
    outs = [tok(POOL_WIDTH, F32), tok(qk, BF16), tok(qk, BF16), vt, tok(LRU_WIDTH, F32), tok(LRU_WIDTH, F32)]
    return pl.pallas_call(
        _inproj_body,
        grid=(n // tm,),
        in_specs=[pl.BlockSpec((tm, d), lambda i: (i, 0)),
                  _const_spec((1, d)),
                  _const_spec(w.shape)],
        out_specs=[spec for spec, _ in outs],
        out_shape=[shape for _, shape in outs],
        compiler_params=_params("parallel"),
        name="inproj",
    )(x2, g.reshape(1, d), w)


POOL_PAD = 16


def _pool_body(p_ref, w_ref, sc_ref, o_ref, pad_ref, *, seq, rows):
    zeros = jnp.zeros((POOL_PAD, POOL_WIDTH), F32)
    pad_ref[0:POOL_PAD, :] = zeros
    pad_ref[seq + POOL_PAD:seq + 2 * POOL_PAD, :] = zeros
    pad_ref[POOL_PAD:seq + POOL_PAD, :] = p_ref[0]

    low = lax.broadcasted_iota(jnp.int32, (rows, LANES), 1) < LANES // 2
    row = lax.broadcasted_iota(jnp.int32, (rows, LANES), 0)

    def doubled(a, d):
        return a[d:, :] + a[:a.shape[0] - d, :]

    def chunk(c, carry):
        t0 = pl.multiple_of(c * rows, rows)
        xs = pad_ref[pl.ds(t0, rows + 2 * POOL_PAD), :]
        xa, xb = xs[:, :LANES], xs[:, LANES:]
        a2 = doubled(xa, 1)
        a4 = doubled(a2, 2)
        b8 = doubled(doubled(doubled(xb, 1), 2), 4)
        b16 = doubled(b8, 8)
        sum_a = jnp.where(low, a2[POOL_PAD - 1:POOL_PAD - 1 + rows], a4[POOL_PAD - 2:POOL_PAD - 2 + rows])
        sum_b = jnp.where(low, b8[POOL_PAD - 4:POOL_PAD - 4 + rows], b16[POOL_PAD - 8:POOL_PAD - 8 + rows])
        t = t0 + row

        def count(half):
            return (jnp.minimum(t + half, seq) - jnp.maximum(t - half, 0)).astype(F32)

        cnt_a = jnp.where(low, count(1), count(2))
        cnt_b = jnp.where(low, count(4), count(8))
        mix_a = sum_a / cnt_a - xa[POOL_PAD:POOL_PAD + rows]
        mix_b = sum_b / cnt_b - xb[POOL_PAD:POOL_PAD + rows]
        mixed = jnp.concatenate([mix_a, mix_b], axis=1).astype(BF16)
        y = _dot(mixed, w_ref[...]) * sc_ref[...]
        o_ref[0, pl.ds(t0, rows), :] = y.astype(BF16)
        return carry

    lax.fori_loop(0, seq // rows, chunk, 0)


def _pool(p3, w_bd, scale, *, rows=512):
    b, s, c = p3.shape
    return pl.pallas_call(
        functools.partial(_pool_body, seq=s, rows=rows),
        grid=(b,),
        in_specs=[pl.BlockSpec((1, s, c), lambda i: (i, 0, 0)),
                  _const_spec(w_bd.shape),
                  _const_spec((1, c))],
        out_specs=pl.BlockSpec((1, s, c), lambda i: (i, 0, 0)),
        out_shape=jax.ShapeDtypeStruct((b, s, c), BF16),
        scratch_shapes=[pltpu.VMEM((s + 2 * POOL_PAD, c), F32)],
        compiler_params=_params("parallel"),
        name="pool",
    )(p3, w_bd, scale.reshape(1, c))


LRU_PAD = 8
SCAN_ROWS = 8


def _softplus(x):
    return jnp.maximum(x, 0.0) + jnp.log1p(jnp.exp(-jnp.abs(x)))


def _lru_body(lx_ref, lg_ref, cw_ref, cb_ref, wg_ref, bg_ref, lam_ref, o_ref,
              pad_ref, af_ref, uf_ref, ab_ref, ub_ref, *, seq, rows):
    w = LRU_WIDTH
    zeros = jnp.zeros((LRU_PAD, w), F32)
    pad_ref[0:LRU_PAD, :] = zeros
    pad_ref[seq + LRU_PAD:seq + 2 * LRU_PAD, :] = zeros
    pad_ref[LRU_PAD:seq + LRU_PAD, :] = lx_ref[0]

    decay = -LRU_C * _softplus(-lam_ref[...])
    lead = LRU_CONV_WIDTH // 2

    def gates(c, carry):
        t0 = pl.multiple_of(c * rows, rows)
        xs = pad_ref[pl.ds(t0, rows + 2 * LRU_PAD), :]
        xf = cb_ref[...]
        for j in range(LRU_CONV_WIDTH):
            s0 = LRU_PAD - lead + j
            xf = xf + cw_ref[j:j + 1, :] * xs[s0:s0 + rows]
        g = _dot(xf.astype(BF16), wg_ref[...]) + bg_ref[...]
        for d, (a_ref, u_ref) in enumerate(((af_ref, uf_ref), (ab_ref, ub_ref))):
            r = jax.nn.sigmoid(g[:, (2 * d) * w:(2 * d + 1) * w])
            i = jax.nn.sigmoid(g[:, (2 * d + 1) * w:(2 * d + 2) * w])
            log_a = decay[d:d + 1, :] * r
            a = jnp.exp(log_a)
            a_ref[pl.ds(t0, rows), :] = a
            one_minus_a2 = -jnp.tanh(log_a) * (a * a + 1.0)
            u_ref[pl.ds(t0, rows), :] = jnp.sqrt(one_minus_a2) * (i * xf)
        return carry

    lax.fori_loop(0, seq // rows, gates, 0)

    def scan(i, carry):
        hf, hb = carry
        base_f = pl.multiple_of(i * SCAN_ROWS, SCAN_ROWS)
        base_b = pl.multiple_of(seq - SCAN_ROWS - i * SCAN_ROWS, SCAN_ROWS)
        for r in range(SCAN_ROWS):
            tf = base_f + r
            tb = base_b + (SCAN_ROWS - 1 - r)
            hf = af_ref[pl.ds(tf, 1), :] * hf + uf_ref[pl.ds(tf, 1), :]
            uf_ref[pl.ds(tf, 1), :] = hf
            hb = ab_ref[pl.ds(tb, 1), :] * hb + ub_ref[pl.ds(tb, 1), :]
            ub_ref[pl.ds(tb, 1), :] = hb
        return hf, hb

    h0 = jnp.zeros((1, w), F32)
    lax.fori_loop(0, seq // SCAN_ROWS, scan, (h0, h0))

    def gate_out(c, carry):
        t0 = pl.multiple_of(c * rows, rows)
        h = uf_ref[pl.ds(t0, rows), :] + ub_ref[pl.ds(t0, rows), :]
        y = jax.nn.gelu(lg_ref[0, pl.ds(t0, rows), :]) * h
        o_ref[0, pl.ds(t0, rows), :] = y.astype(BF16)
        return carry

    lax.fori_loop(0, seq // rows, gate_out, 0)


def _lru(lx3, lg3, conv_w, conv_b, w_gates, b_gates, lam, *, rows=512):
    b, s, c = lx3.shape
    seq_spec = pl.BlockSpec((1, s, c), lambda i: (i, 0, 0))
    return pl.pallas_call(
        functools.partial(_lru_body, seq=s, rows=rows),
        grid=(b,),
        in_specs=[seq_spec, seq_spec,
                  _const_spec(conv_w.shape), _const_spec((1, c)),
                  _const_spec(w_gates.shape), _const_spec((1, 4 * c)),
                  _const_spec(lam.shape)],
        out_specs=seq_spec,
        out_shape=jax.ShapeDtypeStruct((b, s, c), BF16),
        scratch_shapes=[pltpu.VMEM((s + 2 * LRU_PAD, c), F32)]
        + [pltpu.VMEM((s, c), F32) for _ in range(4)],
        compiler_params=_params("parallel"),
        name="lru",
    )(lx3, lg3, conv_w, conv_b.reshape(1, c), w_gates, b_gates.reshape(1, 4 * c), lam)


ALIBI_SPLIT = 64
ONES_ROWS = 16


def _bf16_split3(c):
    pieces = []
    for _ in range(3):
        p = c.astype(BF16).astype(F32)
        pieces.append(p)
        c = c - p
    return pieces


def _alibi_tables(seq):
    assert seq <= ALIBI_SPLIT * 256
    slopes = jnp.asarray([2.0 ** (-8.0 / ATTN_HEADS * (i + 1)) for i in range(ATTN_HEADS)], F32)
    coef = slopes * LOG2E
    pos = jnp.arange(seq)
    parts = [jnp.broadcast_to((ALIBI_SPLIT * (pos // ALIBI_SPLIT)).astype(F32), (ATTN_HEADS, seq)),
             jnp.broadcast_to((pos % ALIBI_SPLIT).astype(F32), (ATTN_HEADS, seq))]
    cs = [jnp.broadcast_to(c[:, None], (ATTN_HEADS, seq)) for c in _bf16_split3(coef)]
    q_cols = [part for part in parts for _ in cs] + [c for _ in parts for c in cs]
    k_cols = [-c for _ in parts for c in cs] + [part for part in parts for _ in cs]
    pad = ((0, 0), (0, 0), (0, LANES - len(q_cols)))
    q_tab = jnp.pad(jnp.stack(q_cols, axis=-1), pad).astype(BF16)
    k_tab = jnp.pad(jnp.stack(k_cols, axis=-1), pad).astype(BF16)
    return coef, q_tab, k_tab


def _attn_body(coef_ref, lam_ref, g_ref, q_ref, qtab_ref, k_ref, ktab_ref, vt_ref, o_ref,
               acc1_ref, acc2_ref, qa_ref, *, tq, tk, seq, lam_init):
    head = pl.program_id(1)
    q0 = pl.program_id(2) * tq
    diag = q0 // tk

    qf = q_ref[0].astype(F32)
    lane = lax.broadcasted_iota(jnp.int32, qf.shape, 1)
    q1 = jnp.where(lane < ATTN_QK_DIM, qf, 0.0).astype(BF16)
    q2 = jnp.where(lane >= ATTN_QK_DIM, qf, 0.0).astype(BF16)
    tab_before = qtab_ref[0]
    tab_after = (-tab_before.astype(F32)).astype(BF16)
    for side, tab in enumerate((tab_before, tab_after)):
        qa_ref[side, 0] = jnp.concatenate([q1, tab], axis=1)
        qa_ref[side, 1] = jnp.concatenate([q2, tab], axis=1)

    acc1_ref[...] = jnp.zeros_like(acc1_ref)
    acc2_ref[...] = jnp.zeros_like(acc2_ref)

    hd = ATTN_V_DIM
    ones = jnp.ones((ONES_ROWS, tk), BF16)

    def update(s, m, acc_ref, k0):
        vt = jnp.concatenate([vt_ref[0, :, pl.ds(k0, tk)], ones], axis=0)
        m_new = jnp.maximum(m, jnp.max(s, axis=0, keepdims=True))
        p = jnp.exp2(s - m_new)
        acc_ref[...] = jnp.exp2(m - m_new) * acc_ref[...] + _dot(vt, p.astype(BF16))
        return m_new

    def folded_scores(jj):
        j = jj + (jj >= diag).astype(jnp.int32)
        side = (j > diag).astype(jnp.int32)
        k0 = pl.multiple_of(j * tk, tk)
        ka = jnp.concatenate([k_ref[0, pl.ds(k0, tk), :], ktab_ref[0, pl.ds(k0, tk), :]], axis=1)
        return _dot_nt(ka, qa_ref[side, 0]), _dot_nt(ka, qa_ref[side, 1]), k0

    def diag_scores():
        k0 = pl.multiple_of(diag * tk, tk)
        kc = k_ref[0, pl.ds(k0, tk), :]
        rel = (k0 - q0 + lax.broadcasted_iota(jnp.int32, (tk, tq), 0)
               - lax.broadcasted_iota(jnp.int32, (tk, tq), 1))
        bias = jnp.abs(rel).astype(F32) * (-coef_ref[head])
        return _dot_nt(kc, q1) + bias, _dot_nt(kc, q2) + bias, k0

    m1 = m2 = jnp.full((1, tq), -jnp.inf, F32)
    cur = diag_scores()
    for jj in range(seq // tk - 1):
        nxt = folded_scores(jnp.int32(jj))
        m1 = update(cur[0], m1, acc1_ref, cur[2])
        m2 = update(cur[1], m2, acc2_ref, cur[2])
        cur = nxt
    update(cur[0], m1, acc1_ref, cur[2])
    update(cur[1], m2, acc2_ref, cur[2])

    lp = lam_ref[...]
    lam = (jnp.exp(jnp.sum(lp[0:1] * lp[1:2], axis=-1, keepdims=True))
           - jnp.exp(jnp.sum(lp[2:3] * lp[3:4], axis=-1, keepdims=True)) + lam_init)
    ot = (acc1_ref[:hd, :] / acc1_ref[hd:hd + 1, :]) - lam * (acc2_ref[:hd, :] / acc2_ref[hd:hd + 1, :])
    ot = ot * lax.rsqrt(jnp.mean(ot * ot, axis=0, keepdims=True) + SUBLN_EPS)
    o_ref[0] = (ot.T * g_ref[...] * (1.0 - lam_init)).astype(BF16)


def _attn(q3, k3, vt3, lam_params, subln_g, lam_init, *, tq=512, tk=512):
    b, s, _ = q3.shape
    hd = ATTN_V_DIM
    assert tk % tq == 0 and s % tk == 0
    coef, q_tab, k_tab = _alibi_tables(s)
    k_spec = pl.BlockSpec((1, s, hd), lambda bi, hi, qi: (bi, 0, hi))
    vt_spec = pl.BlockSpec((1, hd, s), lambda bi, hi, qi: (bi, hi, 0))
    q_spec = pl.BlockSpec((1, tq, hd), lambda bi, hi, qi: (bi, qi, hi))
    return pl.pallas_call(
        functools.partial(_attn_body, tq=tq, tk=tk, seq=s, lam_init=lam_init),
        grid=(b, ATTN_HEADS, s // tq),
        in_specs=[pl.BlockSpec(memory_space=pltpu.SMEM),
                  _const_spec(lam_params.shape),
                  _const_spec((1, hd)),
                  q_spec,
                  pl.BlockSpec((1, tq, LANES), lambda bi, hi, qi: (hi, qi, 0)),
                  k_spec,
                  pl.BlockSpec((1, s, LANES), lambda bi, hi, qi: (hi, 0, 0)),
                  vt_spec],
        out_specs=q_spec,
        out_shape=jax.ShapeDtypeStruct((b, s, ATTN_WIDTH), BF16),
        scratch_shapes=[pltpu.VMEM((hd + ONES_ROWS, tq), F32), pltpu.VMEM((hd + ONES_ROWS, tq), F32),
                        pltpu.VMEM((2, 2, tq, 2 * LANES), BF16)],
        compiler_params=_params("parallel", "parallel", "parallel"),
        name="diff_attn",
    )(coef, lam_params, subln_g.reshape(1, hd), q3, q_tab, k3, k_tab, vt3)


def _merge_body(x_ref, g_ref, wgate_ref, mb_ref, pool_ref, attn_ref, lru_ref,
                wbp_ref, wba_ref, wbl_ref, wout_ref, o_ref):
    x = x_ref[...]
    d = x.shape[-1]
    h = _rms(x, g_ref[...], NORM_EPS).astype(BF16)
    merged = None
    for i, (br_ref, wb_ref) in enumerate(((pool_ref, wbp_ref), (attn_ref, wba_ref), (lru_ref, wbl_ref))):
        logits = _dot(h, wgate_ref[:, i * d:(i + 1) * d]) + mb_ref[i:i + 1, :]
        term = jax.nn.sigmoid(logits) * _dot(br_ref[...], wb_ref[...])
        merged = term if merged is None else merged + term
    o_ref[...] = x + _dot(merged.astype(BF16), wout_ref[...])


def _merge(x2, g, w_gate, merge_bias, pool2, attn2, lru2, wbp, wba, wbl, w_out, *, tm=512):
    n, d = x2.shape

    def tok(width):
        return pl.BlockSpec((tm, width), lambda i: (i, 0))

    return pl.pallas_call(
        _merge_body,
        grid=(n // tm,),
        in_specs=[tok(d), _const_spec((1, d)), _const_spec(w_gate.shape), _const_spec(merge_bias.shape),
                  tok(pool2.shape[1]), tok(attn2.shape[1]), tok(lru2.shape[1]),
                  _const_spec(wbp.shape), _const_spec(wba.shape), _const_spec(wbl.shape),
                  _const_spec(w_out.shape)],
        out_specs=tok(d),
        out_shape=jax.ShapeDtypeStruct((n, d), F32),
        compiler_params=_params("parallel"),
        name="merge",
    )(x2, g.reshape(1, d), w_gate, merge_bias, pool2, attn2, lru2, wbp, wba, wbl, w_out)


def _block_diag(w):
    g, c, _ = w.shape
    eye = jnp.eye(g, dtype=w.dtype)
    return (eye[:, None, :, None] * w[:, :, None, :]).reshape(g * c, g * c)


def kernel(x, ffn1_norm, ffn1_w_in, ffn1_w_out, mix_norm, w_in, pool_w, pool_scale, attn_lambda, attn_subln, lru_conv_w, lru_conv_b, lru_w_a, lru_b_a, lru_w_x, lru_b_x, lru_lambda, w_branch_pool, w_branch_attn, w_branch_lru, merge_bias, w_out, ffn2_norm, ffn2_w_in, ffn2_w_out, final_norm):
    b, s, d = x.shape
    depth = ffn1_norm.shape[0]
    qk = ATTN_HEADS * 2 * ATTN_QK_DIM
    mix_cols = POOL_WIDTH + 2 * qk + ATTN_WIDTH + 2 * LRU_WIDTH
    x2 = x.reshape(b * s, d)
    for l in range(depth):
        lam_init = 0.8 - 0.6 * math.exp(-0.3 * l)
        last = l == depth - 1
        x2 = _ffn(x2, ffn1_norm[l], ffn1_w_in[l].astype(BF16), ffn1_w_out[l].astype(BF16))

        p, q, k, vt, lx, lg = _inproj(x2, mix_norm[l], w_in[l, :, :mix_cols].astype(BF16), s)
        pool = _pool(p.reshape(b, s, -1), _block_diag(pool_w[l]).astype(BF16), pool_scale[l])
        w_gates = jnp.concatenate(
            [_block_diag(lru_w_a[l, 0]), _block_diag(lru_w_x[l, 0]),
             _block_diag(lru_w_a[l, 1]), _block_diag(lru_w_x[l, 1])], axis=1).astype(BF16)
        b_gates = jnp.concatenate([lru_b_a[l, 0], lru_b_x[l, 0], lru_b_a[l, 1], lru_b_x[l, 1]])
        lru = _lru(lx.reshape(b, s, -1), lg.reshape(b, s, -1), lru_conv_w[l], lru_conv_b[l],
                   w_gates, b_gates, lru_lambda[l])
        attn = _attn(q.reshape(b, s, -1), k.reshape(b, s, -1), vt,
                     attn_lambda[l], attn_subln[l], lam_init)
        x2 = _merge(x2, mix_norm[l], w_in[l, :, mix_cols:].astype(BF16), merge_bias[l],
                    pool.reshape(b * s, -1), attn.reshape(b * s, -1), lru.reshape(b * s, -1),
                    w_branch_pool[l].astype(BF16), w_branch_attn[l].astype(BF16),
                    w_branch_lru[l].astype(BF16), w_out[l].astype(BF16))

        x2 = _ffn(x2, ffn2_norm[l], ffn2_w_in[l].astype(BF16), ffn2_w_out[l].astype(BF16),
                  final_norm if last else None)
    return x2.reshape(b, s, d)
```

```python
import functools
import math

import jax
import jax.numpy as jnp
from jax import lax
from jax.experimental import pallas as pl
from jax.experimental.pallas import tpu as pltpu

F32 = jnp.float32
BF16 = jnp.bfloat16

NORM_EPS = 1e-6
SUBLN_EPS = 1e-5
LRU_C = 8.0

POOL_WINDOWS = (2, 4, 8, 16)
POOL_WIDTH = 256
ATTN_HEADS = 4
ATTN_QK_DIM = 64
ATTN_V_DIM = 128
ATTN_WIDTH = ATTN_HEADS * ATTN_V_DIM
LRU_WIDTH = 256
LRU_CONV_WIDTH = 4

LOG2E = math.log2(math.e)
Q_SCALE = ATTN_QK_DIM ** -0.5 * LOG2E

LANES = 128
MXU_COLS = 256
VMEM_LIMIT_BYTES = 52 * 1024 * 1024


def _params(*sem):
    return pltpu.CompilerParams(dimension_semantics=sem,
                                vmem_limit_bytes=VMEM_LIMIT_BYTES)


def _const_spec(shape):
    nd = len(shape)
    return pl.BlockSpec(shape, lambda *_: (0,) * nd, pipeline_mode=pl.Buffered(1))


def _rms(x, g, eps):
    ms = jnp.mean(x * x, axis=-1, keepdims=True)
    return x * lax.rsqrt(ms + eps) * g


def _dot(a, b):
    return jnp.dot(a, b, preferred_element_type=F32)


def _dot_nt(a, b):
    return lax.dot_general(a, b, (((1,), (1,)), ((), ())), preferred_element_type=F32)


def _ffn_body(*refs, d_ff, chunk, final):
    if final:
        x_ref, g_ref, win_ref, wout_ref, fg_ref, o_ref, acc_ref = refs
    else:
        x_ref, g_ref, win_ref, wout_ref, o_ref, acc_ref = refs
    x = x_ref[...]
    h = _rms(x, g_ref[...], NORM_EPS).astype(BF16)
    for c in range(d_ff // chunk):
        lo = c * chunk
        gate = _dot(h, win_ref[:, lo:lo + chunk])
        up = _dot(h, win_ref[:, d_ff + lo:d_ff + lo + chunk])
        act = (gate * jax.nn.sigmoid(gate) * up).astype(BF16)
        part = _dot(act, wout_ref[lo:lo + chunk, :])
        if c == 0:
            acc_ref[...] = part
        else:
            acc_ref[...] += part
    y = x + 0.5 * acc_ref[...]
    if final:
        y = _rms(y, fg_ref[...], NORM_EPS)
    o_ref[...] = y


def _ffn(x2, g, w_in, w_out, final_g=None, *, tm=512):
    n, d = x2.shape
    d_ff = w_out.shape[0]
    final = final_g is not None
    in_specs = [
        pl.BlockSpec((tm, d), lambda i: (i, 0)),
        _const_spec((1, d)),
        _const_spec(w_in.shape),
        _const_spec(w_out.shape),
    ]
    args = [x2, g.reshape(1, d), w_in, w_out]
    if final:
        in_specs.append(_const_spec((1, d)))
        args.append(final_g.reshape(1, d))
    return pl.pallas_call(
        functools.partial(_ffn_body, d_ff=d_ff, chunk=MXU_COLS, final=final),
        grid=(n // tm,),
        in_specs=in_specs,
        out_specs=pl.BlockSpec((tm, d), lambda i: (i, 0)),
        out_shape=jax.ShapeDtypeStruct((n, d), F32),
        scratch_shapes=[pltpu.VMEM((tm, d), F32)],
        compiler_params=_params("parallel"),
        name="ffn_final" if final else "ffn",
    )(*args)


def _inproj_body(x_ref, g_ref, w_ref, p_ref, q_ref, k_ref, vt_ref, lx_ref, lg_ref):
    h = _rms(x_ref[...], g_ref[...], NORM_EPS).astype(BF16)
    qk = ATTN_HEADS * 2 * ATTN_QK_DIM
    off = 0

    def proj(width):
        nonlocal off
        out = _dot(h, w_ref[:, off:off + width])
        off += width
        return out

    p_ref[...] = proj(POOL_WIDTH)
    q_ref[...] = (proj(qk) * Q_SCALE).astype(BF16)
    k_ref[...] = proj(qk).astype(BF16)
    vt_ref[0] = proj(ATTN_WIDTH).T.astype(BF16)
    lx_ref[...] = proj(LRU_WIDTH)
    lg_ref[...] = proj(LRU_WIDTH)


def _inproj(x2, g, w, seq, *, tm=512):
    n, d = x2.shape
    assert seq % tm == 0
    tiles = seq // tm
    qk = ATTN_HEADS * 2 * ATTN_QK_DIM

    def tok(width, dtype):
        return pl.BlockSpec((tm, width), lambda i: (i, 0)), jax.ShapeDtypeStruct((n, width), dtype)

    vt = (pl.BlockSpec((1, ATTN_WIDTH, tm), lambda i: (i // tiles, 0, i % tiles)),
          jax.ShapeDtypeStruct((n // seq, ATTN_WIDTH, seq), BF16))
    outs = [tok(POOL_WIDTH, F32), tok(qk, BF16), tok(qk, BF16), vt, tok(LRU_WIDTH, F32), tok(LRU_WIDTH, F32)]
    return pl.pallas_call(
        _inproj_body,
        grid=(n // tm,),
        in_specs=[pl.BlockSpec((tm, d), lambda i: (i, 0)),
                  _const_spec((1, d)),
                  _const_spec(w.shape)],
        out_specs=[spec for spec, _ in outs],
        out_shape=[shape for _, shape in outs],
        compiler_params=_params("parallel"),
        name="inproj",
    )(x2, g.reshape(1, d), w)


POOL_PAD = 16


def _pool_body(p_ref, w_ref, sc_ref, o_ref, pad_ref, *, seq, rows):
    zeros = jnp.zeros((POOL_PAD, POOL_WIDTH), F32)
    pad_ref[0:POOL_PAD, :] = zeros
    pad_ref[seq + POOL_PAD:seq + 2 * POOL_PAD, :] = zeros
    pad_ref[POOL_PAD:seq + POOL_PAD, :] = p_ref[0]

    low = lax.broadcasted_iota(jnp.int32, (rows, LANES), 1) < LANES // 2
    row = lax.broadcasted_iota(jnp.int32, (rows, LANES), 0)

    def doubled(a, d):
        return a[d:, :] + a[:a.shape[0] - d, :]

    def chunk(c, carry):
        t0 = pl.multiple_of(c * rows, rows)
        xs = pad_ref[pl.ds(t0, rows + 2 * POOL_PAD), :]
        xa, xb = xs[:, :LANES], xs[:, LANES:]
        a2 = doubled(xa, 1)
        a4 = doubled(a2, 2)
        b8 = doubled(doubled(doubled(xb, 1), 2), 4)
        b16 = doubled(b8, 8)
        sum_a = jnp.where(low, a2[POOL_PAD - 1:POOL_PAD - 1 + rows], a4[POOL_PAD - 2:POOL_PAD - 2 + rows])
        sum_b = jnp.where(low, b8[POOL_PAD - 4:POOL_PAD - 4 + rows], b16[POOL_PAD - 8:POOL_PAD - 8 + rows])
        t = t0 + row

        def count(half):
            return (jnp.minimum(t + half, seq) - jnp.maximum(t - half, 0)).astype(F32)

        cnt_a = jnp.where(low, count(1), count(2))
        cnt_b = jnp.where(low, count(4), count(8))
        mix_a = sum_a / cnt_a - xa[POOL_PAD:POOL_PAD + rows]
        mix_b = sum_b / cnt_b - xb[POOL_PAD:POOL_PAD + rows]
        mixed = jnp.concatenate([mix_a, mix_b], axis=1).astype(BF16)
        y = _dot(mixed, w_ref[...]) * sc_ref[...]
        o_ref[0, pl.ds(t0, rows), :] = y.astype(BF16)
        return carry

    lax.fori_loop(0, seq // rows, chunk, 0)


def _pool(p3, w_bd, scale, *, rows=512):
    b, s, c = p3.shape
    return pl.pallas_call(
        functools.partial(_pool_body, seq=s, rows=rows),
        grid=(b,),
        in_specs=[pl.BlockSpec((1, s, c), lambda i: (i, 0, 0)),
                  _const_spec(w_bd.shape),
                  _const_spec((1, c))],
        out_specs=pl.BlockSpec((1, s, c), lambda i: (i, 0, 0)),
        out_shape=jax.ShapeDtypeStruct((b, s, c), BF16),
        scratch_shapes=[pltpu.VMEM((s + 2 * POOL_PAD, c), F32)],
        compiler_params=_params("parallel"),
        name="pool",
    )(p3, w_bd, scale.reshape(1, c))


LRU_PAD = 8
SCAN_ROWS = 8


def _softplus(x):
    return jnp.maximum(x, 0.0) + jnp.log1p(jnp.exp(-jnp.abs(x)))


def _lru_body(lx_ref, lg_ref, cw_ref, cb_ref, wg_ref, bg_ref, lam_ref, o_ref,
              pad_ref, af_ref, uf_ref, ab_ref, ub_ref, *, seq, rows):
    w = LRU_WIDTH
    zeros = jnp.zeros((LRU_PAD, w), F32)
    pad_ref[0:LRU_PAD, :] = zeros
    pad_ref[seq + LRU_PAD:seq + 2 * LRU_PAD, :] = zeros
    pad_ref[LRU_PAD:seq + LRU_PAD, :] = lx_ref[0]

    decay = -LRU_C * _softplus(-lam_ref[...])
    lead = LRU_CONV_WIDTH // 2
    row_in_blk = lax.broadcasted_iota(jnp.int32, (rows, w), 0) % SCAN_ROWS

    def blk_shift(x, d, fill, reverse):
        blocks = x.reshape(rows // SCAN_ROWS, SCAN_ROWS, w)
        if reverse:
            moved = pltpu.roll(blocks, SCAN_ROWS - d, axis=1).reshape(rows, w)
            return jnp.where(row_in_blk < SCAN_ROWS - d, moved, fill)
        moved = pltpu.roll(blocks, d, axis=1).reshape(rows, w)
        return jnp.where(row_in_blk >= d, moved, fill)

    def gates(c, carry):
        t0 = pl.multiple_of(c * rows, rows)
        xs = pad_ref[pl.ds(t0, rows + 2 * LRU_PAD), :]
        xf = cb_ref[...]
        for j in range(LRU_CONV_WIDTH):
            s0 = LRU_PAD - lead + j
            xf = xf + cw_ref[j:j + 1, :] * xs[s0:s0 + rows]
        g = _dot(xf.astype(BF16), wg_ref[...]) + bg_ref[...]
        for d, (a_ref, u_ref) in enumerate(((af_ref, uf_ref), (ab_ref, ub_ref))):
            r = jax.nn.sigmoid(g[:, (2 * d) * w:(2 * d + 1) * w])
            i = jax.nn.sigmoid(g[:, (2 * d + 1) * w:(2 * d + 2) * w])
            log_a = decay[d:d + 1, :] * r
            a = jnp.exp(log_a)
            one_minus_a2 = -jnp.tanh(log_a) * (a * a + 1.0)
            u = jnp.sqrt(one_minus_a2) * (i * xf)
            step = 1
            while step < SCAN_ROWS:
                u = u + a * blk_shift(u, step, 0.0, d == 1)
                a = a * blk_shift(a, step, 1.0, d == 1)
                step *= 2
            a_ref[pl.ds(t0, rows), :] = a
            u_ref[pl.ds(t0, rows), :] = u
        return carry

    lax.fori_loop(0, seq // rows, gates, 0)

    def carry_blocks(i, carry):
        hf, hb = carry
        base_f = pl.multiple_of(i * SCAN_ROWS, SCAN_ROWS)
        base_b = pl.multiple_of(seq - SCAN_ROWS - i * SCAN_ROWS, SCAN_ROWS)
        tf = uf_ref[pl.ds(base_f, SCAN_ROWS), :] + af_ref[pl.ds(base_f, SCAN_ROWS), :] * hf
        uf_ref[pl.ds(base_f, SCAN_ROWS), :] = tf
        tb = ub_ref[pl.ds(base_b, SCAN_ROWS), :] + ab_ref[pl.ds(base_b, SCAN_ROWS), :] * hb
        ub_ref[pl.ds(base_b, SCAN_ROWS), :] = tb
        return tf[SCAN_ROWS - 1:SCAN_ROWS, :], tb[0:1, :]

    h0 = jnp.zeros((1, w), F32)
    lax.fori_loop(0, seq // SCAN_ROWS, carry_blocks, (h0, h0), unroll=8)

    def gate_out(c, carry):
        t0 = pl.multiple_of(c * rows, rows)
        h = uf_ref[pl.ds(t0, rows), :] + ub_ref[pl.ds(t0, rows), :]
        y = jax.nn.gelu(lg_ref[0, pl.ds(t0, rows), :]) * h
        o_ref[0, pl.ds(t0, rows), :] = y.astype(BF16)
        return carry

    lax.fori_loop(0, seq // rows, gate_out, 0)


def _lru(lx3, lg3, conv_w, conv_b, w_gates, b_gates, lam, *, rows=512):
    b, s, c = lx3.shape
    seq_spec = pl.BlockSpec((1, s, c), lambda i: (i, 0, 0))
    return pl.pallas_call(
        functools.partial(_lru_body, seq=s, rows=rows),
        grid=(b,),
        in_specs=[seq_spec, seq_spec,
                  _const_spec(conv_w.shape), _const_spec((1, c)),
                  _const_spec(w_gates.shape), _const_spec((1, 4 * c)),
                  _const_spec(lam.shape)],
        out_specs=seq_spec,
        out_shape=jax.ShapeDtypeStruct((b, s, c), BF16),
        scratch_shapes=[pltpu.VMEM((s + 2 * LRU_PAD, c), F32)]
        + [pltpu.VMEM((s, c), F32) for _ in range(4)],
        compiler_params=_params("parallel"),
        name="lru",
    )(lx3, lg3, conv_w, conv_b.reshape(1, c), w_gates, b_gates.reshape(1, 4 * c), lam)


ALIBI_SPLIT = 64
ONES_ROWS = 16

def _bf16_split3(c):
    pieces = []
    for _ in range(3):
        p = c.astype(BF16).astype(F32)
        pieces.append(p)
        c = c - p
    return pieces


def _alibi_tables(seq):
    assert seq <= ALIBI_SPLIT * 256
    slopes = jnp.asarray([2.0 ** (-8.0 / ATTN_HEADS * (i + 1)) for i in range(ATTN_HEADS)], F32)
    coef = slopes * LOG2E
    pos = jnp.arange(seq)
    parts = [jnp.broadcast_to((ALIBI_SPLIT * (pos // ALIBI_SPLIT)).astype(F32), (ATTN_HEADS, seq)),
             jnp.broadcast_to((pos % ALIBI_SPLIT).astype(F32), (ATTN_HEADS, seq))]
    cs = [jnp.broadcast_to(c[:, None], (ATTN_HEADS, seq)) for c in _bf16_split3(coef)]
    q_cols = [part for part in parts for _ in cs] + [c for _ in parts for c in cs]
    k_cols = [-c for _ in parts for c in cs] + [part for part in parts for _ in cs]
    pad = ((0, 0), (0, 0), (0, LANES - len(q_cols)))
    q_tab = jnp.pad(jnp.stack(q_cols, axis=-1), pad).astype(BF16)
    k_tab = jnp.pad(jnp.stack(k_cols, axis=-1), pad).astype(BF16)
    return coef, q_tab, k_tab


def _attn_body(coef_ref, lam_ref, g_ref, q_ref, qtab_ref, k_ref, ktab_ref, vt_ref, o_ref,
               acc1_ref, acc2_ref, qa_ref, *, tq, tk, seq, lam_init):
    head = pl.program_id(1)
    q0 = pl.program_id(2) * tq
    diag = q0 // tk

    qf = q_ref[0].astype(F32)
    lane = lax.broadcasted_iota(jnp.int32, qf.shape, 1)
    q1 = jnp.where(lane < ATTN_QK_DIM, qf, 0.0).astype(BF16)
    q2 = jnp.where(lane >= ATTN_QK_DIM, qf, 0.0).astype(BF16)
    tab_before = qtab_ref[0]
    tab_after = (-tab_before.astype(F32)).astype(BF16)
    for side, tab in enumerate((tab_before, tab_after)):
        qa_ref[side, 0] = jnp.concatenate([q1, tab], axis=1)
        qa_ref[side, 1] = jnp.concatenate([q2, tab], axis=1)

    acc1_ref[...] = jnp.zeros_like(acc1_ref)
    acc2_ref[...] = jnp.zeros_like(acc2_ref)

    hd = ATTN_V_DIM
    ones = jnp.ones((ONES_ROWS, tk), BF16)

    def update(s, m, acc_ref, k0):
        vt = jnp.concatenate([vt_ref[0, :, pl.ds(k0, tk)], ones], axis=0)
        m_new = jnp.maximum(m, jnp.max(s, axis=0, keepdims=True))
        p = jnp.exp2(s - m_new)
        acc_ref[...] = jnp.exp2(m - m_new) * acc_ref[...] + _dot(vt, p.astype(BF16))
        return m_new

    def folded_scores(jj):
        j = jj + (jj >= diag).astype(jnp.int32)
        side = (j > diag).astype(jnp.int32)
        k0 = pl.multiple_of(j * tk, tk)
        ka = jnp.concatenate([k_ref[0, pl.ds(k0, tk), :], ktab_ref[0, pl.ds(k0, tk), :]], axis=1)
        return _dot_nt(ka, qa_ref[side, 0]), _dot_nt(ka, qa_ref[side, 1]), k0

    def diag_scores():
        k0 = pl.multiple_of(diag * tk, tk)
        kc = k_ref[0, pl.ds(k0, tk), :]
        rel = (k0 - q0 + lax.broadcasted_iota(jnp.int32, (tk, tq), 0)
               - lax.broadcasted_iota(jnp.int32, (tk, tq), 1))
        bias = jnp.abs(rel).astype(F32) * (-coef_ref[head])
        return _dot_nt(kc, q1) + bias, _dot_nt(kc, q2) + bias, k0

    m1 = m2 = jnp.full((1, tq), -jnp.inf, F32)
    cur = diag_scores()
    for jj in range(seq // tk - 1):
        nxt = folded_scores(jnp.int32(jj))
        m1 = update(cur[0], m1, acc1_ref, cur[2])
        m2 = update(cur[1], m2, acc2_ref, cur[2])
        cur = nxt
    update(cur[0], m1, acc1_ref, cur[2])
    update(cur[1], m2, acc2_ref, cur[2])

    lp = lam_ref[...]
    lam = (jnp.exp(jnp.sum(lp[0:1] * lp[1:2], axis=-1, keepdims=True))
           - jnp.exp(jnp.sum(lp[2:3] * lp[3:4], axis=-1, keepdims=True)) + lam_init)
    ot = (acc1_ref[:hd, :] / acc1_ref[hd:hd + 1, :]) - lam * (acc2_ref[:hd, :] / acc2_ref[hd:hd + 1, :])
    ot = ot * lax.rsqrt(jnp.mean(ot * ot, axis=0, keepdims=True) + SUBLN_EPS)
    o_ref[0] = (ot.T * g_ref[...] * (1.0 - lam_init)).astype(BF16)


def _attn(q3, k3, vt3, lam_params, subln_g, lam_init, *, tq=512, tk=512):
    b, s, _ = q3.shape
    hd = ATTN_V_DIM
    assert tk % tq == 0 and s % tk == 0
    coef, q_tab, k_tab = _alibi_tables(s)
    k_spec = pl.BlockSpec((1, s, hd), lambda bi, hi, qi: (bi, 0, hi))
    vt_spec = pl.BlockSpec((1, hd, s), lambda bi, hi, qi: (bi, hi, 0))
    q_spec = pl.BlockSpec((1, tq, hd), lambda bi, hi, qi: (bi, qi, hi))
    return pl.pallas_call(
        functools.partial(_attn_body, tq=tq, tk=tk, seq=s, lam_init=lam_init),
        grid=(b, ATTN_HEADS, s // tq),
        in_specs=[pl.BlockSpec(memory_space=pltpu.SMEM),
                  _const_spec(lam_params.shape),
                  _const_spec((1, hd)),
                  q_spec,
                  pl.BlockSpec((1, tq, LANES), lambda bi, hi, qi: (hi, qi, 0)),
                  k_spec,
                  pl.BlockSpec((1, s, LANES), lambda bi, hi, qi: (hi, 0, 0)),
                  vt_spec],
        out_specs=q_spec,
        out_shape=jax.ShapeDtypeStruct((b, s, ATTN_WIDTH), BF16),
        scratch_shapes=[pltpu.VMEM((hd + ONES_ROWS, tq), F32), pltpu.VMEM((hd + ONES_ROWS, tq), F32),
                        pltpu.VMEM((2, 2, tq, 2 * LANES), BF16)],
        compiler_params=_params("parallel", "parallel", "parallel"),
        name="diff_attn",
    )(coef, lam_params, subln_g.reshape(1, hd), q3, q_tab, k3, k_tab, vt3)


def _merge_body(x_ref, g_ref, wgate_ref, mb_ref, pool_ref, attn_ref, lru_ref,
                wbp_ref, wba_ref, wbl_ref, wout_ref, o_ref):
    x = x_ref[...]
    d = x.shape[-1]
    h = _rms(x, g_ref[...], NORM_EPS).astype(BF16)
    merged = None
    for i, (br_ref, wb_ref) in enumerate(((pool_ref, wbp_ref), (attn_ref, wba_ref), (lru_ref, wbl_ref))):
        logits = _dot(h, wgate_ref[:, i * d:(i + 1) * d]) + mb_ref[i:i + 1, :]
        term = jax.nn.sigmoid(logits) * _dot(br_ref[...], wb_ref[...])
        merged = term if merged is None else merged + term
    o_ref[...] = x + _dot(merged.astype(BF16), wout_ref[...])


def _merge(x2, g, w_gate, merge_bias, pool2, attn2, lru2, wbp, wba, wbl, w_out, *, tm=512):
    n, d = x2.shape

    def tok(width):
        return pl.BlockSpec((tm, width), lambda i: (i, 0))

    return pl.pallas_call(
        _merge_body,
        grid=(n // tm,),
        in_specs=[tok(d), _const_spec((1, d)), _const_spec(w_gate.shape), _const_spec(merge_bias.shape),
                  tok(pool2.shape[1]), tok(attn2.shape[1]), tok(lru2.shape[1]),
                  _const_spec(wbp.shape), _const_spec(wba.shape), _const_spec(wbl.shape),
                  _const_spec(w_out.shape)],
        out_specs=tok(d),
        out_shape=jax.ShapeDtypeStruct((n, d), F32),
        compiler_params=_params("parallel"),
        name="merge",
    )(x2, g.reshape(1, d), w_gate, merge_bias, pool2, attn2, lru2, wbp, wba, wbl, w_out)


def _block_diag(w):
    g, c, _ = w.shape
    eye = jnp.eye(g, dtype=w.dtype)
    return (eye[:, None, :, None] * w[:, :, None, :]).reshape(g * c, g * c)


def kernel(x, ffn1_norm, ffn1_w_in, ffn1_w_out, mix_norm, w_in, pool_w, pool_scale, attn_lambda, attn_subln, lru_conv_w, lru_conv_b, lru_w_a, lru_b_a, lru_w_x, lru_b_x, lru_lambda, w_branch_pool, w_branch_attn, w_branch_lru, merge_bias, w_out, ffn2_norm, ffn2_w_in, ffn2_w_out, final_norm):
    b, s, d = x.shape
    depth = ffn1_norm.shape[0]
    qk = ATTN_HEADS * 2 * ATTN_QK_DIM
    mix_cols = POOL_WIDTH + 2 * qk + ATTN_WIDTH + 2 * LRU_WIDTH
    x2 = x.reshape(b * s, d)
    for l in range(depth):
        lam_init = 0.8 - 0.6 * math.exp(-0.3 * l)
        last = l == depth - 1
        x2 = _ffn(x2, ffn1_norm[l], ffn1_w_in[l].astype(BF16), ffn1_w_out[l].astype(BF16))

        p, q, k, vt, lx, lg = _inproj(x2, mix_norm[l], w_in[l, :, :mix_cols].astype(BF16), s)
        pool = _pool(p.reshape(b, s, -1), _block_diag(pool_w[l]).astype(BF16), pool_scale[l])
        w_gates = jnp.concatenate(
            [_block_diag(lru_w_a[l, 0]), _block_diag(lru_w_x[l, 0]),
             _block_diag(lru_w_a[l, 1]), _block_diag(lru_w_x[l, 1])], axis=1).astype(BF16)
        b_gates = jnp.concatenate([lru_b_a[l, 0], lru_b_x[l, 0], lru_b_a[l, 1], lru_b_x[l, 1]])
        lru = _lru(lx.reshape(b, s, -1), lg.reshape(b, s, -1), lru_conv_w[l], lru_conv_b[l],
                   w_gates, b_gates, lru_lambda[l])
        attn = _attn(q.reshape(b, s, -1), k.reshape(b, s, -1), vt,
                     attn_lambda[l], attn_subln[l], lam_init)
        x2 = _merge(x2, mix_norm[l], w_in[l, :, mix_cols:].astype(BF16), merge_bias[l],
                    pool.reshape(b * s, -1), attn.reshape(b * s, -1), lru.reshape(b * s, -1),
                    w_branch_pool[l].astype(BF16), w_branch_attn[l].astype(BF16),
                    w_branch_lru[l].astype(BF16), w_out[l].astype(BF16))

        x2 = _ffn(x2, ffn2_norm[l], ffn2_w_in[l].astype(BF16), ffn2_w_out[l].astype(BF16),
                  final_norm if last else None)
    return x2.reshape(b, s, d)
```

```python
import functools
import math

import jax
import jax.numpy as jnp
from jax import lax
from jax.experimental import pallas as pl
from jax.experimental.pallas import tpu as pltpu

F32 = jnp.float32
BF16 = jnp.bfloat16

NORM_EPS = 1e-6
SUBLN_EPS = 1e-5
LRU_C = 8.0

POOL_WINDOWS = (2, 4, 8, 16)
POOL_WIDTH = 256
ATTN_HEADS = 4
ATTN_QK_DIM = 64
ATTN_V_DIM = 128
ATTN_WIDTH = ATTN_HEADS * ATTN_V_DIM
LRU_WIDTH = 256
LRU_CONV_WIDTH = 4

LOG2E = math.log2(math.e)
Q_SCALE = ATTN_QK_DIM ** -0.5 * LOG2E

LANES = 128
MXU_COLS = 256
VMEM_LIMIT_BYTES = 52 * 1024 * 1024


def _params(*sem):
    return pltpu.CompilerParams(dimension_semantics=sem,
                                vmem_limit_bytes=VMEM_LIMIT_BYTES)


def _const_spec(shape):
    nd = len(shape)
    return pl.BlockSpec(shape, lambda *_: (0,) * nd, pipeline_mode=pl.Buffered(1))


def _rms(x, g, eps):
    ms = jnp.mean(x * x, axis=-1, keepdims=True)
    return x * lax.rsqrt(ms + eps) * g


def _dot(a, b):
    return jnp.dot(a, b, preferred_element_type=F32)


def _dot_nt(a, b):
    return lax.dot_general(a, b, (((1,), (1,)), ((), ())), preferred_element_type=F32)


def _ffn_body(*refs, d_ff, chunk, final):
    if final:
        x_ref, g_ref, win_ref, wout_ref, fg_ref, o_ref, acc_ref = refs
    else:
        x_ref, g_ref, win_ref, wout_ref, o_ref, acc_ref = refs
    x = x_ref[...]
    h = _rms(x, g_ref[...], NORM_EPS).astype(BF16)
    for c in range(d_ff // chunk):
        lo = c * chunk
        gate = _dot(h, win_ref[:, lo:lo + chunk])
        up = _dot(h, win_ref[:, d_ff + lo:d_ff + lo + chunk])
        act = (gate * jax.nn.sigmoid(gate) * up).astype(BF16)
        part = _dot(act, wout_ref[lo:lo + chunk, :])
        if c == 0:
            acc_ref[...] = part
        else:
            acc_ref[...] += part
    y = x + 0.5 * acc_ref[...]
    if final:
        y = _rms(y, fg_ref[...], NORM_EPS)
    o_ref[...] = y


def _ffn(x2, g, w_in, w_out, final_g=None, *, tm=512):
    n, d = x2.shape
    d_ff = w_out.shape[0]
    final = final_g is not None
    in_specs = [
        pl.BlockSpec((tm, d), lambda i: (i, 0)),
        _const_spec((1, d)),
        _const_spec(w_in.shape),
        _const_spec(w_out.shape),
    ]
    args = [x2, g.reshape(1, d), w_in, w_out]
    if final:
        in_specs.append(_const_spec((1, d)))
        args.append(final_g.reshape(1, d))
    return pl.pallas_call(
        functools.partial(_ffn_body, d_ff=d_ff, chunk=MXU_COLS, final=final),
        grid=(n // tm,),
        in_specs=in_specs,
        out_specs=pl.BlockSpec((tm, d), lambda i: (i, 0)),
        out_shape=jax.ShapeDtypeStruct((n, d), F32),
        scratch_shapes=[pltpu.VMEM((tm, d), F32)],
        compiler_params=_params("parallel"),
        name="ffn_final" if final else "ffn",
    )(*args)


def _inproj_body(x_ref, g_ref, w_ref, p_ref, q_ref, k_ref, vt_ref, lx_ref, lg_ref):
    h = _rms(x_ref[...], g_ref[...], NORM_EPS).astype(BF16)
    qk = ATTN_HEADS * 2 * ATTN_QK_DIM
    off = 0

    def proj(width):
        nonlocal off
        out = _dot(h, w_ref[:, off:off + width])
        off += width
        return out

    p_ref[...] = proj(POOL_WIDTH)
    q_ref[...] = (proj(qk) * Q_SCALE).astype(BF16)
    k_ref[...] = proj(qk).astype(BF16)
    vt_ref[0] = proj(ATTN_WIDTH).T.astype(BF16)
    lx_ref[...] = proj(LRU_WIDTH)
    lg_ref[...] = proj(LRU_WIDTH)


def _inproj(x2, g, w, seq, *, tm=512):
    n, d = x2.shape
    assert seq % tm == 0
    tiles = seq // tm
    qk = ATTN_HEADS * 2 * ATTN_QK_DIM

    def tok(width, dtype):
        return pl.BlockSpec((tm, width), lambda i: (i, 0)), jax.ShapeDtypeStruct((n, width), dtype)

    vt = (pl.BlockSpec((1, ATTN_WIDTH, tm), lambda i: (i // tiles, 0, i % tiles)),
          jax.ShapeDtypeStruct((n // seq, ATTN_WIDTH, seq), BF16))
    outs = [tok(POOL_WIDTH, F32), tok(qk, BF16), tok(qk, BF16), vt, tok(LRU_WIDTH, F32), tok(LRU_WIDTH, F32)]
    return pl.pallas_call(
        _inproj_body,
        grid=(n // tm,),
        in_specs=[pl.BlockSpec((tm, d), lambda i: (i, 0)),
                  _const_spec((1, d)),
                  _const_spec(w.shape)],
        out_specs=[spec for spec, _ in outs],
        out_shape=[shape for _, shape in outs],
        compiler_params=_params("parallel"),
        name="inproj",
    )(x2, g.reshape(1, d), w)


POOL_PAD = 16


def _pool_body(p_ref, w_ref, sc_ref, o_ref, pad_ref, *, seq, rows):
    zeros = jnp.zeros((POOL_PAD, POOL_WIDTH), F32)
    pad_ref[0:POOL_PAD, :] = zeros
    pad_ref[seq + POOL_PAD:seq + 2 * POOL_PAD, :] = zeros
    pad_ref[POOL_PAD:seq + POOL_PAD, :] = p_ref[0]

    low = lax.broadcasted_iota(jnp.int32, (rows, LANES), 1) < LANES // 2
    row = lax.broadcasted_iota(jnp.int32, (rows, LANES), 0)

    def doubled(a, d):
        return a[d:, :] + a[:a.shape[0] - d, :]

    def chunk(c, carry):
        t0 = pl.multiple_of(c * rows, rows)
        xs = pad_ref[pl.ds(t0, rows + 2 * POOL_PAD), :]
        xa, xb = xs[:, :LANES], xs[:, LANES:]
        a2 = doubled(xa, 1)
        a4 = doubled(a2, 2)
        b8 = doubled(doubled(doubled(xb, 1), 2), 4)
        b16 = doubled(b8, 8)
        sum_a = jnp.where(low, a2[POOL_PAD - 1:POOL_PAD - 1 + rows], a4[POOL_PAD - 2:POOL_PAD - 2 + rows])
        sum_b = jnp.where(low, b8[POOL_PAD - 4:POOL_PAD - 4 + rows], b16[POOL_PAD - 8:POOL_PAD - 8 + rows])
        t = t0 + row

        def count(half):
            return (jnp.minimum(t + half, seq) - jnp.maximum(t - half, 0)).astype(F32)

        cnt_a = jnp.where(low, count(1), count(2))
        cnt_b = jnp.where(low, count(4), count(8))
        mix_a = sum_a / cnt_a - xa[POOL_PAD:POOL_PAD + rows]
        mix_b = sum_b / cnt_b - xb[POOL_PAD:POOL_PAD + rows]
        mixed = jnp.concatenate([mix_a, mix_b], axis=1).astype(BF16)
        y = _dot(mixed, w_ref[...]) * sc_ref[...]
        o_ref[0, pl.ds(t0, rows), :] = y.astype(BF16)
        return carry

    lax.fori_loop(0, seq // rows, chunk, 0)


def _pool(p3, w_bd, scale, *, rows=512):
    b, s, c = p3.shape
    return pl.pallas_call(
        functools.partial(_pool_body, seq=s, rows=rows),
        grid=(b,),
        in_specs=[pl.BlockSpec((1, s, c), lambda i: (i, 0, 0)),
                  _const_spec(w_bd.shape),
                  _const_spec((1, c))],
        out_specs=pl.BlockSpec((1, s, c), lambda i: (i, 0, 0)),
        out_shape=jax.ShapeDtypeStruct((b, s, c), BF16),
        scratch_shapes=[pltpu.VMEM((s + 2 * POOL_PAD, c), F32)],
        compiler_params=_params("parallel"),
        name="pool",
    )(p3, w_bd, scale.reshape(1, c))


LRU_PAD = 8
SCAN_ROWS = 8


def _softplus(x):
    return jnp.maximum(x, 0.0) + jnp.log1p(jnp.exp(-jnp.abs(x)))


def _lru_body(lx_ref, lg_ref, cw_ref, cb_ref, wg_ref, bg_ref, lam_ref, o_ref,
              pad_ref, af_ref, uf_ref, ab_ref, ub_ref, *, seq, rows):
    w = LRU_WIDTH
    zeros = jnp.zeros((LRU_PAD, w), F32)
    pad_ref[0:LRU_PAD, :] = zeros
    pad_ref[seq + LRU_PAD:seq + 2 * LRU_PAD, :] = zeros
    pad_ref[LRU_PAD:seq + LRU_PAD, :] = lx_ref[0]

    decay = -LRU_C * _softplus(-lam_ref[...])
    lead = LRU_CONV_WIDTH // 2
    row_in_blk = lax.broadcasted_iota(jnp.int32, (rows, w), 0) % SCAN_ROWS

    def blk_shift(x, d, fill, reverse):
        blocks = x.reshape(rows // SCAN_ROWS, SCAN_ROWS, w)
        if reverse:
            moved = pltpu.roll(blocks, SCAN_ROWS - d, axis=1).reshape(rows, w)
            return jnp.where(row_in_blk < SCAN_ROWS - d, moved, fill)
        moved = pltpu.roll(blocks, d, axis=1).reshape(rows, w)
        return jnp.where(row_in_blk >= d, moved, fill)

    def gates(c, carry):
        t0 = pl.multiple_of(c * rows, rows)
        xs = pad_ref[pl.ds(t0, rows + 2 * LRU_PAD), :]
        xf = cb_ref[...]
        for j in range(LRU_CONV_WIDTH):
            s0 = LRU_PAD - lead + j
            xf = xf + cw_ref[j:j + 1, :] * xs[s0:s0 + rows]
        g = _dot(xf.astype(BF16), wg_ref[...]) + bg_ref[...]
        for d, (a_ref, u_ref) in enumerate(((af_ref, uf_ref), (ab_ref, ub_ref))):
            r = jax.nn.sigmoid(g[:, (2 * d) * w:(2 * d + 1) * w])
            i = jax.nn.sigmoid(g[:, (2 * d + 1) * w:(2 * d + 2) * w])
            log_a = decay[d:d + 1, :] * r
            a = jnp.exp(log_a)
            one_minus_a2 = -jnp.tanh(log_a) * (a * a + 1.0)
            u = jnp.sqrt(one_minus_a2) * (i * xf)
            step = 1
            while step < SCAN_ROWS:
                u = u + a * blk_shift(u, step, 0.0, d == 1)
                a = a * blk_shift(a, step, 1.0, d == 1)
                step *= 2
            a_ref[pl.ds(t0, rows), :] = a
            u_ref[pl.ds(t0, rows), :] = u
        return carry

    lax.fori_loop(0, seq // rows, gates, 0)

    def carry_blocks(i, carry):
        hf, hb = carry
        base_f = pl.multiple_of(i * SCAN_ROWS, SCAN_ROWS)
        base_b = pl.multiple_of(seq - SCAN_ROWS - i * SCAN_ROWS, SCAN_ROWS)
        tf = uf_ref[pl.ds(base_f, SCAN_ROWS), :] + af_ref[pl.ds(base_f, SCAN_ROWS), :] * hf
        uf_ref[pl.ds(base_f, SCAN_ROWS), :] = tf
        tb = ub_ref[pl.ds(base_b, SCAN_ROWS), :] + ab_ref[pl.ds(base_b, SCAN_ROWS), :] * hb
        ub_ref[pl.ds(base_b, SCAN_ROWS), :] = tb
        return tf[SCAN_ROWS - 1:SCAN_ROWS, :], tb[0:1, :]

    h0 = jnp.zeros((1, w), F32)
    lax.fori_loop(0, seq // SCAN_ROWS, carry_blocks, (h0, h0), unroll=8)

    def gate_out(c, carry):
        t0 = pl.multiple_of(c * rows, rows)
        h = uf_ref[pl.ds(t0, rows), :] + ub_ref[pl.ds(t0, rows), :]
        y = jax.nn.gelu(lg_ref[0, pl.ds(t0, rows), :]) * h
        o_ref[0, pl.ds(t0, rows), :] = y.astype(BF16)
        return carry

    lax.fori_loop(0, seq // rows, gate_out, 0)


def _lru(lx3, lg3, conv_w, conv_b, w_gates, b_gates, lam, *, rows=512):
    b, s, c = lx3.shape
    seq_spec = pl.BlockSpec((1, s, c), lambda i: (i, 0, 0))
    return pl.pallas_call(
        functools.partial(_lru_body, seq=s, rows=rows),
        grid=(b,),
        in_specs=[seq_spec, seq_spec,
                  _const_spec(conv_w.shape), _const_spec((1, c)),
                  _const_spec(w_gates.shape), _const_spec((1, 4 * c)),
                  _const_spec(lam.shape)],
        out_specs=seq_spec,
        out_shape=jax.ShapeDtypeStruct((b, s, c), BF16),
        scratch_shapes=[pltpu.VMEM((s + 2 * LRU_PAD, c), F32)]
        + [pltpu.VMEM((s, c), F32) for _ in range(4)],
        compiler_params=_params("parallel"),
        name="lru",
    )(lx3, lg3, conv_w, conv_b.reshape(1, c), w_gates, b_gates.reshape(1, 4 * c), lam)


ALIBI_SPLIT = 64
ONES_ROWS = 16
SKEW = 2

def _bf16_split3(c):
    pieces = []
    for _ in range(3):
        p = c.astype(BF16).astype(F32)
        pieces.append(p)
        c = c - p
    return pieces


def _alibi_tables(seq):
    assert seq <= ALIBI_SPLIT * 256
    slopes = jnp.asarray([2.0 ** (-8.0 / ATTN_HEADS * (i + 1)) for i in range(ATTN_HEADS)], F32)
    coef = slopes * LOG2E
    pos = jnp.arange(seq)
    parts = [jnp.broadcast_to((ALIBI_SPLIT * (pos // ALIBI_SPLIT)).astype(F32), (ATTN_HEADS, seq)),
             jnp.broadcast_to((pos % ALIBI_SPLIT).astype(F32), (ATTN_HEADS, seq))]
    cs = [jnp.broadcast_to(c[:, None], (ATTN_HEADS, seq)) for c in _bf16_split3(coef)]
    q_cols = [part for part in parts for _ in cs] + [c for _ in parts for c in cs]
    k_cols = [-c for _ in parts for c in cs] + [part for part in parts for _ in cs]
    pad = ((0, 0), (0, 0), (0, LANES - len(q_cols)))
    q_tab = jnp.pad(jnp.stack(q_cols, axis=-1), pad).astype(BF16)
    k_tab = jnp.pad(jnp.stack(k_cols, axis=-1), pad).astype(BF16)
    return coef, q_tab, k_tab


def _attn_body(lam_ref, g_ref, q_ref, qtab_ref, dbias_ref, k_ref, ktab_ref, vt_ref, o_ref,
               acc1_ref, acc2_ref, *, tq, tk, seq, lam_init):
    diag = pl.program_id(2)

    qf = q_ref[0].astype(F32)
    lane = lax.broadcasted_iota(jnp.int32, qf.shape, 1)
    q1 = jnp.where(lane < ATTN_QK_DIM, qf, 0.0).astype(BF16)
    q2 = jnp.where(lane >= ATTN_QK_DIM, qf, 0.0).astype(BF16)
    tab_before = qtab_ref[0]
    tab_after = (-tab_before.astype(F32)).astype(BF16)

    acc1_ref[...] = jnp.zeros_like(acc1_ref)
    acc2_ref[...] = jnp.zeros_like(acc2_ref)

    hd = ATTN_V_DIM
    ones = jnp.ones((ONES_ROWS, tk), BF16)

    def update(s, m, acc_ref, k0):
        vt = jnp.concatenate([vt_ref[0, :, pl.ds(k0, tk)], ones], axis=0)
        m_new = jnp.maximum(m, jnp.max(s, axis=0, keepdims=True))
        p = jnp.exp2(s - m_new)
        acc_ref[...] = jnp.exp2(m - m_new) * acc_ref[...] + _dot(vt, p.astype(BF16))
        return m_new

    def folded_scores(jj):
        j = jj + (jj >= diag).astype(jnp.int32)
        tab = jnp.where(j > diag, tab_after, tab_before)
        k0 = pl.multiple_of(j * tk, tk)
        ka = jnp.concatenate([k_ref[0, pl.ds(k0, tk), :], ktab_ref[0, pl.ds(k0, tk), :]], axis=1)
        return (_dot_nt(ka, jnp.concatenate([q1, tab], axis=1)),
                _dot_nt(ka, jnp.concatenate([q2, tab], axis=1)), k0)

    def diag_scores():
        k0 = pl.multiple_of(diag * tk, tk)
        kc = k_ref[0, pl.ds(k0, tk), :]
        return _dot_nt(kc, q1) + dbias_ref[0], _dot_nt(kc, q2) + dbias_ref[0], k0

    m1 = m2 = jnp.full((1, tq), -jnp.inf, F32)
    pending = [diag_scores()]
    for jj in range(seq // tk - 1):
        pending.append(folded_scores(jnp.int32(jj)))
        if len(pending) > SKEW:
            s1, s2, k0 = pending.pop(0)
            m1 = update(s1, m1, acc1_ref, k0)
            m2 = update(s2, m2, acc2_ref, k0)
    for s1, s2, k0 in pending:
        m1 = update(s1, m1, acc1_ref, k0)
        m2 = update(s2, m2, acc2_ref, k0)

    lp = lam_ref[...]
    lam = (jnp.exp(jnp.sum(lp[0:1] * lp[1:2], axis=-1, keepdims=True))
           - jnp.exp(jnp.sum(lp[2:3] * lp[3:4], axis=-1, keepdims=True)) + lam_init)
    inv1 = 1.0 / acc1_ref[hd:hd + 1, :]
    inv2 = lam / acc2_ref[hd:hd + 1, :]
    ot = acc1_ref[:hd, :] * inv1 - acc2_ref[:hd, :] * inv2
    ot = ot * lax.rsqrt(jnp.mean(ot * ot, axis=0, keepdims=True) + SUBLN_EPS)
    o_ref[0] = (ot.T * g_ref[...] * (1.0 - lam_init)).astype(BF16)


def _attn(q3, k3, vt3, lam_params, subln_g, lam_init, *, tq=512, tk=512):
    b, s, _ = q3.shape
    hd = ATTN_V_DIM
    assert tk == tq and s % tk == 0
    coef, q_tab, k_tab = _alibi_tables(s)
    dist = jnp.abs(jnp.arange(tk)[:, None] - jnp.arange(tq)[None, :]).astype(F32)
    diag_bias = -coef[:, None, None] * dist
    k_spec = pl.BlockSpec((1, s, hd), lambda bi, hi, qi: (bi, 0, hi))
    vt_spec = pl.BlockSpec((1, hd, s), lambda bi, hi, qi: (bi, hi, 0))
    q_spec = pl.BlockSpec((1, tq, hd), lambda bi, hi, qi: (bi, qi, hi))
    return pl.pallas_call(
        functools.partial(_attn_body, tq=tq, tk=tk, seq=s, lam_init=lam_init),
        grid=(b, ATTN_HEADS, s // tq),
        in_specs=[_const_spec(lam_params.shape),
                  _const_spec((1, hd)),
                  q_spec,
                  pl.BlockSpec((1, tq, LANES), lambda bi, hi, qi: (hi, qi, 0)),
                  pl.BlockSpec((1, tk, tq), lambda bi, hi, qi: (hi, 0, 0)),
                  k_spec,
                  pl.BlockSpec((1, s, LANES), lambda bi, hi, qi: (hi, 0, 0)),
                  vt_spec],
        out_specs=q_spec,
        out_shape=jax.ShapeDtypeStruct((b, s, ATTN_WIDTH), BF16),
        scratch_shapes=[pltpu.VMEM((hd + ONES_ROWS, tq), F32), pltpu.VMEM((hd + ONES_ROWS, tq), F32)],
        compiler_params=_params("parallel", "parallel", "parallel"),
        name="diff_attn",
    )(lam_params, subln_g.reshape(1, hd), q3, q_tab, diag_bias, k3, k_tab, vt3)


def _merge_body(x_ref, g_ref, wgate_ref, mb_ref, pool_ref, attn_ref, lru_ref,
                wbp_ref, wba_ref, wbl_ref, wout_ref, o_ref):
    x = x_ref[...]
    d = x.shape[-1]
    h = _rms(x, g_ref[...], NORM_EPS).astype(BF16)
    merged = None
    for i, (br_ref, wb_ref) in enumerate(((pool_ref, wbp_ref), (attn_ref, wba_ref), (lru_ref, wbl_ref))):
        logits = _dot(h, wgate_ref[:, i * d:(i + 1) * d]) + mb_ref[i:i + 1, :]
        term = jax.nn.sigmoid(logits) * _dot(br_ref[...], wb_ref[...])
        merged = term if merged is None else merged + term
    o_ref[...] = x + _dot(merged.astype(BF16), wout_ref[...])


def _merge(x2, g, w_gate, merge_bias, pool2, attn2, lru2, wbp, wba, wbl, w_out, *, tm=512):
    n, d = x2.shape

    def tok(width):
        return pl.BlockSpec((tm, width), lambda i: (i, 0))

    return pl.pallas_call(
        _merge_body,
        grid=(n // tm,),
        in_specs=[tok(d), _const_spec((1, d)), _const_spec(w_gate.shape), _const_spec(merge_bias.shape),
                  tok(pool2.shape[1]), tok(attn2.shape[1]), tok(lru2.shape[1]),
                  _const_spec(wbp.shape), _const_spec(wba.shape), _const_spec(wbl.shape),
                  _const_spec(w_out.shape)],
        out_specs=tok(d),
        out_shape=jax.ShapeDtypeStruct((n, d), F32),
        compiler_params=_params("parallel"),
        name="merge",
    )(x2, g.reshape(1, d), w_gate, merge_bias, pool2, attn2, lru2, wbp, wba, wbl, w_out)


def _block_diag(w):
    g, c, _ = w.shape
    eye = jnp.eye(g, dtype=w.dtype)
    return (eye[:, None, :, None] * w[:, :, None, :]).reshape(g * c, g * c)


def kernel(x, ffn1_norm, ffn1_w_in, ffn1_w_out, mix_norm, w_in, pool_w, pool_scale, attn_lambda, attn_subln, lru_conv_w, lru_conv_b, lru_w_a, lru_b_a, lru_w_x, lru_b_x, lru_lambda, w_branch_pool, w_branch_attn, w_branch_lru, merge_bias, w_out, ffn2_norm, ffn2_w_in, ffn2_w_out, final_norm):
    b, s, d = x.shape
    depth = ffn1_norm.shape[0]
    qk = ATTN_HEADS * 2 * ATTN_QK_DIM
    mix_cols = POOL_WIDTH + 2 * qk + ATTN_WIDTH + 2 * LRU_WIDTH
    x2 = x.reshape(b * s, d)
    for l in range(depth):
        lam_init = 0.8 - 0.6 * math.exp(-0.3 * l)
        last = l == depth - 1
        x2 = _ffn(x2, ffn1_norm[l], ffn1_w_in[l].astype(BF16), ffn1_w_out[l].astype(BF16))

        p, q, k, vt, lx, lg = _inproj(x2, mix_norm[l], w_in[l, :, :mix_cols].astype(BF16), s)
        pool = _pool(p.reshape(b, s, -1), _block_diag(pool_w[l]).astype(BF16), pool_scale[l])
        w_gates = jnp.concatenate(
            [_block_diag(lru_w_a[l, 0]), _block_diag(lru_w_x[l, 0]),
             _block_diag(lru_w_a[l, 1]), _block_diag(lru_w_x[l, 1])], axis=1).astype(BF16)
        b_gates = jnp.concatenate([lru_b_a[l, 0], lru_b_x[l, 0], lru_b_a[l, 1], lru_b_x[l, 1]])
        lru = _lru(lx.reshape(b, s, -1), lg.reshape(b, s, -1), lru_conv_w[l], lru_conv_b[l],
                   w_gates, b_gates, lru_lambda[l])
        attn = _attn(q.reshape(b, s, -1), k.reshape(b, s, -1), vt,
                     attn_lambda[l], attn_subln[l], lam_init)
        x2 = _merge(x2, mix_norm[l], w_in[l, :, mix_cols:].astype(BF16), merge_bias[l],
                    pool.reshape(b * s, -1), attn.reshape(b * s, -1), lru.reshape(b * s, -1),
                    w_branch_pool[l].astype(BF16), w_branch_attn[l].astype(BF16),
                    w_branch_lru[l].astype(BF16), w_out[l].astype(BF16))

        x2 = _ffn(x2, ffn2_norm[l], ffn2_w_in[l].astype(BF16), ffn2_w_out[l].astype(BF16),
                  final_norm if last else None)
    return x2.reshape(b, s, d)
```

```python
import functools
import math

import jax
import jax.numpy as jnp
from jax import lax
from jax.experimental import pallas as pl
from jax.experimental.pallas import tpu as pltpu

F32 = jnp.float32
BF16 = jnp.bfloat16

NORM_EPS = 1e-6
SUBLN_EPS = 1e-5
LRU_C = 8.0

POOL_WINDOWS = (2, 4, 8, 16)
POOL_WIDTH = 256
ATTN_HEADS = 4
ATTN_QK_DIM = 64
ATTN_V_DIM = 128
ATTN_WIDTH = ATTN_HEADS * ATTN_V_DIM
LRU_WIDTH = 256
LRU_CONV_WIDTH = 4

LOG2E = math.log2(math.e)
Q_SCALE = ATTN_QK_DIM ** -0.5 * LOG2E

LANES = 128
MXU_COLS = 256
VMEM_LIMIT_BYTES = 52 * 1024 * 1024


def _params(*sem):
    return pltpu.CompilerParams(dimension_semantics=sem,
                                vmem_limit_bytes=VMEM_LIMIT_BYTES)


def _const_spec(shape):
    nd = len(shape)
    return pl.BlockSpec(shape, lambda *_: (0,) * nd, pipeline_mode=pl.Buffered(1))


def _rms(x, g, eps):
    ms = jnp.mean(x * x, axis=-1, keepdims=True)
    return x * lax.rsqrt(ms + eps) * g


def _dot(a, b):
    return jnp.dot(a, b, preferred_element_type=F32)


def _dot_nt(a, b):
    return lax.dot_general(a, b, (((1,), (1,)), ((), ())), preferred_element_type=F32)


def _ffn_body(*refs, d_ff, chunk, final):
    if final:
        x_ref, g_ref, win_ref, wout_ref, fg_ref, o_ref, acc_ref = refs
    else:
        x_ref, g_ref, win_ref, wout_ref, o_ref, acc_ref = refs
    x = x_ref[...]
    h = _rms(x, g_ref[...], NORM_EPS).astype(BF16)
    for c in range(d_ff // chunk):
        lo = c * chunk
        gate = _dot(h, win_ref[:, lo:lo + chunk])
        up = _dot(h, win_ref[:, d_ff + lo:d_ff + lo + chunk])
        act = (gate * jax.nn.sigmoid(gate) * up).astype(BF16)
        part = _dot(act, wout_ref[lo:lo + chunk, :])
        if c == 0:
            acc_ref[...] = part
        else:
            acc_ref[...] += part
    y = x + 0.5 * acc_ref[...]
    if final:
        y = _rms(y, fg_ref[...], NORM_EPS)
    o_ref[...] = y


def _ffn(x2, g, w_in, w_out, final_g=None, *, tm=512):
    n, d = x2.shape
    d_ff = w_out.shape[0]
    final = final_g is not None
    in_specs = [
        pl.BlockSpec((tm, d), lambda i: (i, 0)),
        _const_spec((1, d)),
        _const_spec(w_in.shape),
        _const_spec(w_out.shape),
    ]
    args = [x2, g.reshape(1, d), w_in, w_out]
    if final:
        in_specs.append(_const_spec((1, d)))
        args.append(final_g.reshape(1, d))
    return pl.pallas_call(
        functools.partial(_ffn_body, d_ff=d_ff, chunk=MXU_COLS, final=final),
        grid=(n // tm,),
        in_specs=in_specs,
        out_specs=pl.BlockSpec((tm, d), lambda i: (i, 0)),
        out_shape=jax.ShapeDtypeStruct((n, d), F32),
        scratch_shapes=[pltpu.VMEM((tm, d), F32)],
        compiler_params=_params("parallel"),
        name="ffn_final" if final else "ffn",
    )(*args)


def _inproj_body(x_ref, g_ref, w_ref, p_ref, q_ref, k_ref, vt_ref, lx_ref, lg_ref):
    h = _rms(x_ref[...], g_ref[...], NORM_EPS).astype(BF16)
    qk = ATTN_HEADS * 2 * ATTN_QK_DIM
    off = 0

    def proj(width):
        nonlocal off
        out = _dot(h, w_ref[:, off:off + width])
        off += width
        return out

    p_ref[...] = proj(POOL_WIDTH)
    q_ref[...] = (proj(qk) * Q_SCALE).astype(BF16)
    k_ref[...] = proj(qk).astype(BF16)
    vt_ref[0] = proj(ATTN_WIDTH).T.astype(BF16)
    lx_ref[...] = proj(LRU_WIDTH)
    lg_ref[...] = proj(LRU_WIDTH)


def _inproj(x2, g, w, seq, *, tm=512):
    n, d = x2.shape
    assert seq % tm == 0
    tiles = seq // tm
    qk = ATTN_HEADS * 2 * ATTN_QK_DIM

    def tok(width, dtype):
        return pl.BlockSpec((tm, width), lambda i: (i, 0)), jax.ShapeDtypeStruct((n, width), dtype)

    vt = (pl.BlockSpec((1, ATTN_WIDTH, tm), lambda i: (i // tiles, 0, i % tiles)),
          jax.ShapeDtypeStruct((n // seq, ATTN_WIDTH, seq), BF16))
    outs = [tok(POOL_WIDTH, F32), tok(qk, BF16), tok(qk, BF16), vt, tok(LRU_WIDTH, F32), tok(LRU_WIDTH, F32)]
    return pl.pallas_call(
        _inproj_body,
        grid=(n // tm,),
        in_specs=[pl.BlockSpec((tm, d), lambda i: (i, 0)),
                  _const_spec((1, d)),
                  _const_spec(w.shape)],
        out_specs=[spec for spec, _ in outs],
        out_shape=[shape for _, shape in outs],
        compiler_params=_params("parallel"),
        name="inproj",
    )(x2, g.reshape(1, d), w)


POOL_PAD = 16


def _pool_body(p_ref, w_ref, sc_ref, o_ref, pad_ref, *, seq, rows):
    zeros = jnp.zeros((POOL_PAD, POOL_WIDTH), F32)
    pad_ref[0:POOL_PAD, :] = zeros
    pad_ref[seq + POOL_PAD:seq + 2 * POOL_PAD, :] = zeros
    pad_ref[POOL_PAD:seq + POOL_PAD, :] = p_ref[0]

    low = lax.broadcasted_iota(jnp.int32, (rows, LANES), 1) < LANES // 2
    row = lax.broadcasted_iota(jnp.int32, (rows, LANES), 0)

    def doubled(a, d):
        return a[d:, :] + a[:a.shape[0] - d, :]

    def chunk(c, carry):
        t0 = pl.multiple_of(c * rows, rows)
        xs = pad_ref[pl.ds(t0, rows + 2 * POOL_PAD), :]
        xa, xb = xs[:, :LANES], xs[:, LANES:]
        a2 = doubled(xa, 1)
        a4 = doubled(a2, 2)
        b8 = doubled(doubled(doubled(xb, 1), 2), 4)
        b16 = doubled(b8, 8)
        sum_a = jnp.where(low, a2[POOL_PAD - 1:POOL_PAD - 1 + rows], a4[POOL_PAD - 2:POOL_PAD - 2 + rows])
        sum_b = jnp.where(low, b8[POOL_PAD - 4:POOL_PAD - 4 + rows], b16[POOL_PAD - 8:POOL_PAD - 8 + rows])
        t = t0 + row

        def count(half):
            return (jnp.minimum(t + half, seq) - jnp.maximum(t - half, 0)).astype(F32)

        cnt_a = jnp.where(low, count(1), count(2))
        cnt_b = jnp.where(low, count(4), count(8))
        mix_a = sum_a / cnt_a - xa[POOL_PAD:POOL_PAD + rows]
        mix_b = sum_b / cnt_b - xb[POOL_PAD:POOL_PAD + rows]
        mixed = jnp.concatenate([mix_a, mix_b], axis=1).astype(BF16)
        y = _dot(mixed, w_ref[...]) * sc_ref[...]
        o_ref[0, pl.ds(t0, rows), :] = y.astype(BF16)
        return carry

    lax.fori_loop(0, seq // rows, chunk, 0)


def _pool(p3, w_bd, scale, *, rows=512):
    b, s, c = p3.shape
    return pl.pallas_call(
        functools.partial(_pool_body, seq=s, rows=rows),
        grid=(b,),
        in_specs=[pl.BlockSpec((1, s, c), lambda i: (i, 0, 0)),
                  _const_spec(w_bd.shape),
                  _const_spec((1, c))],
        out_specs=pl.BlockSpec((1, s, c), lambda i: (i, 0, 0)),
        out_shape=jax.ShapeDtypeStruct((b, s, c), BF16),
        scratch_shapes=[pltpu.VMEM((s + 2 * POOL_PAD, c), F32)],
        compiler_params=_params("parallel"),
        name="pool",
    )(p3, w_bd, scale.reshape(1, c))


LRU_PAD = 8
SCAN_ROWS = 8


def _softplus(x):
    return jnp.maximum(x, 0.0) + jnp.log1p(jnp.exp(-jnp.abs(x)))


def _lru_body(lx_ref, lg_ref, cw_ref, cb_ref, wg_ref, bg_ref, lam_ref, o_ref,
              pad_ref, af_ref, uf_ref, ab_ref, ub_ref, *, seq, rows):
    w = LRU_WIDTH
    zeros = jnp.zeros((LRU_PAD, w), F32)
    pad_ref[0:LRU_PAD, :] = zeros
    pad_ref[seq + LRU_PAD:seq + 2 * LRU_PAD, :] = zeros
    pad_ref[LRU_PAD:seq + LRU_PAD, :] = lx_ref[0]

    decay = -LRU_C * _softplus(-lam_ref[...])
    lead = LRU_CONV_WIDTH // 2
    row_in_blk = lax.broadcasted_iota(jnp.int32, (rows, w), 0) % SCAN_ROWS

    def blk_shift(x, d, fill, reverse):
        blocks = x.reshape(rows // SCAN_ROWS, SCAN_ROWS, w)
        if reverse:
            moved = pltpu.roll(blocks, SCAN_ROWS - d, axis=1).reshape(rows, w)
            return jnp.where(row_in_blk < SCAN_ROWS - d, moved, fill)
        moved = pltpu.roll(blocks, d, axis=1).reshape(rows, w)
        return jnp.where(row_in_blk >= d, moved, fill)

    def gates(c, carry):
        t0 = pl.multiple_of(c * rows, rows)
        xs = pad_ref[pl.ds(t0, rows + 2 * LRU_PAD), :]
        xf = cb_ref[...]
        for j in range(LRU_CONV_WIDTH):
            s0 = LRU_PAD - lead + j
            xf = xf + cw_ref[j:j + 1, :] * xs[s0:s0 + rows]
        g = _dot(xf.astype(BF16), wg_ref[...]) + bg_ref[...]
        for d, (a_ref, u_ref) in enumerate(((af_ref, uf_ref), (ab_ref, ub_ref))):
            r = jax.nn.sigmoid(g[:, (2 * d) * w:(2 * d + 1) * w])
            i = jax.nn.sigmoid(g[:, (2 * d + 1) * w:(2 * d + 2) * w])
            log_a = decay[d:d + 1, :] * r
            a = jnp.exp(log_a)
            one_minus_a2 = -jnp.tanh(log_a) * (a * a + 1.0)
            u = jnp.sqrt(one_minus_a2) * (i * xf)
            step = 1
            while step < SCAN_ROWS:
                u = u + a * blk_shift(u, step, 0.0, d == 1)
                a = a * blk_shift(a, step, 1.0, d == 1)
                step *= 2
            a_ref[pl.ds(t0, rows), :] = a
            u_ref[pl.ds(t0, rows), :] = u
        return carry

    lax.fori_loop(0, seq // rows, gates, 0)

    def carry_blocks(i, carry):
        hf, hb = carry
        base_f = pl.multiple_of(i * SCAN_ROWS, SCAN_ROWS)
        base_b = pl.multiple_of(seq - SCAN_ROWS - i * SCAN_ROWS, SCAN_ROWS)
        tf = uf_ref[pl.ds(base_f, SCAN_ROWS), :] + af_ref[pl.ds(base_f, SCAN_ROWS), :] * hf
        uf_ref[pl.ds(base_f, SCAN_ROWS), :] = tf
        tb = ub_ref[pl.ds(base_b, SCAN_ROWS), :] + ab_ref[pl.ds(base_b, SCAN_ROWS), :] * hb
        ub_ref[pl.ds(base_b, SCAN_ROWS), :] = tb
        return tf[SCAN_ROWS - 1:SCAN_ROWS, :], tb[0:1, :]

    h0 = jnp.zeros((1, w), F32)
    lax.fori_loop(0, seq // SCAN_ROWS, carry_blocks, (h0, h0), unroll=8)

    def gate_out(c, carry):
        t0 = pl.multiple_of(c * rows, rows)
        h = uf_ref[pl.ds(t0, rows), :] + ub_ref[pl.ds(t0, rows), :]
        y = jax.nn.gelu(lg_ref[0, pl.ds(t0, rows), :]) * h
        o_ref[0, pl.ds(t0, rows), :] = y.astype(BF16)
        return carry

    lax.fori_loop(0, seq // rows, gate_out, 0)


def _lru(lx3, lg3, conv_w, conv_b, w_gates, b_gates, lam, *, rows=512):
    b, s, c = lx3.shape
    seq_spec = pl.BlockSpec((1, s, c), lambda i: (i, 0, 0))
    return pl.pallas_call(
        functools.partial(_lru_body, seq=s, rows=rows),
        grid=(b,),
        in_specs=[seq_spec, seq_spec,
                  _const_spec(conv_w.shape), _const_spec((1, c)),
                  _const_spec(w_gates.shape), _const_spec((1, 4 * c)),
                  _const_spec(lam.shape)],
        out_specs=seq_spec,
        out_shape=jax.ShapeDtypeStruct((b, s, c), BF16),
        scratch_shapes=[pltpu.VMEM((s + 2 * LRU_PAD, c), F32)]
        + [pltpu.VMEM((s, c), F32) for _ in range(4)],
        compiler_params=_params("parallel"),
        name="lru",
    )(lx3, lg3, conv_w, conv_b.reshape(1, c), w_gates, b_gates.reshape(1, 4 * c), lam)


ALIBI_SPLIT = 64
ONES_ROWS = 16
SKEW = 2

def _bf16_split3(c):
    pieces = []
    for _ in range(3):
        p = c.astype(BF16).astype(F32)
        pieces.append(p)
        c = c - p
    return pieces


def _alibi_tables(seq):
    assert seq <= ALIBI_SPLIT * 256
    slopes = jnp.asarray([2.0 ** (-8.0 / ATTN_HEADS * (i + 1)) for i in range(ATTN_HEADS)], F32)
    coef = slopes * LOG2E
    pos = jnp.arange(seq)
    parts = [jnp.broadcast_to((ALIBI_SPLIT * (pos // ALIBI_SPLIT)).astype(F32), (ATTN_HEADS, seq)),
             jnp.broadcast_to((pos % ALIBI_SPLIT).astype(F32), (ATTN_HEADS, seq))]
    cs = [jnp.broadcast_to(c[:, None], (ATTN_HEADS, seq)) for c in _bf16_split3(coef)]
    q_cols = [part for part in parts for _ in cs] + [c for _ in parts for c in cs]
    k_cols = [-c for _ in parts for c in cs] + [part for part in parts for _ in cs]
    pad = ((0, 0), (0, 0), (0, LANES - len(q_cols)))
    q_tab = jnp.pad(jnp.stack(q_cols, axis=-1), pad).astype(BF16)
    k_tab = jnp.pad(jnp.stack(k_cols, axis=-1), pad).astype(BF16)
    return coef, q_tab, k_tab


def _attn_body(lam_ref, g_ref, q_ref, qtab_ref, dbias_ref, k_ref, ktab_ref, vt_ref, o_ref,
               acc1_ref, acc2_ref, *, tq, tk, seq, lam_init):
    n_diag = tq // tk
    diag = pl.program_id(2) * n_diag

    qf = q_ref[0].astype(F32)
    lane = lax.broadcasted_iota(jnp.int32, qf.shape, 1)
    q1 = jnp.where(lane < ATTN_QK_DIM, qf, 0.0).astype(BF16)
    q2 = jnp.where(lane >= ATTN_QK_DIM, qf, 0.0).astype(BF16)
    tab_before = qtab_ref[0]
    tab_after = (-tab_before.astype(F32)).astype(BF16)

    acc1_ref[...] = jnp.zeros_like(acc1_ref)
    acc2_ref[...] = jnp.zeros_like(acc2_ref)

    hd = ATTN_V_DIM
    ones = jnp.ones((ONES_ROWS, tk), BF16)

    def update(s, m, acc_ref, k0):
        vt = jnp.concatenate([vt_ref[0, :, pl.ds(k0, tk)], ones], axis=0)
        m_new = jnp.maximum(m, jnp.max(s, axis=0, keepdims=True))
        p = jnp.exp2(s - m_new)
        acc_ref[...] = jnp.exp2(m - m_new) * acc_ref[...] + _dot(vt, p.astype(BF16))
        return m_new

    def folded_scores(jj):
        j = jj + n_diag * (jj >= diag).astype(jnp.int32)
        tab = jnp.where(j > diag, tab_after, tab_before)
        k0 = pl.multiple_of(j * tk, tk)
        ka = jnp.concatenate([k_ref[0, pl.ds(k0, tk), :], ktab_ref[0, pl.ds(k0, tk), :]], axis=1)
        return (_dot_nt(ka, jnp.concatenate([q1, tab], axis=1)),
                _dot_nt(ka, jnp.concatenate([q2, tab], axis=1)), k0)

    def diag_scores(r):
        k0 = pl.multiple_of((diag + r) * tk, tk)
        kc = k_ref[0, pl.ds(k0, tk), :]
        return _dot_nt(kc, q1) + dbias_ref[0, r], _dot_nt(kc, q2) + dbias_ref[0, r], k0

    m1 = m2 = jnp.full((1, tq), -jnp.inf, F32)
    pending = []
    for step in range(seq // tk):
        pending.append(diag_scores(step) if step < n_diag else folded_scores(jnp.int32(step - n_diag)))
        if len(pending) > SKEW:
            s1, s2, k0 = pending.pop(0)
            m1 = update(s1, m1, acc1_ref, k0)
            m2 = update(s2, m2, acc2_ref, k0)
    for s1, s2, k0 in pending:
        m1 = update(s1, m1, acc1_ref, k0)
        m2 = update(s2, m2, acc2_ref, k0)

    lp = lam_ref[...]
    lam = (jnp.exp(jnp.sum(lp[0:1] * lp[1:2], axis=-1, keepdims=True))
           - jnp.exp(jnp.sum(lp[2:3] * lp[3:4], axis=-1, keepdims=True)) + lam_init)
    inv1 = 1.0 / acc1_ref[hd:hd + 1, :]
    inv2 = lam / acc2_ref[hd:hd + 1, :]
    ot = acc1_ref[:hd, :] * inv1 - acc2_ref[:hd, :] * inv2
    ot = ot * lax.rsqrt(jnp.mean(ot * ot, axis=0, keepdims=True) + SUBLN_EPS)
    o_ref[0] = (ot.T * g_ref[...] * (1.0 - lam_init)).astype(BF16)


def _attn(q3, k3, vt3, lam_params, subln_g, lam_init, *, tq=1024, tk=512):
    b, s, _ = q3.shape
    hd = ATTN_V_DIM
    assert tq % tk == 0 and s % tq == 0
    n_diag = tq // tk
    coef, q_tab, k_tab = _alibi_tables(s)
    key_pos = (jnp.arange(n_diag)[:, None, None] * tk + jnp.arange(tk)[None, :, None])
    dist = jnp.abs(key_pos - jnp.arange(tq)[None, None, :]).astype(F32)
    diag_bias = -coef[:, None, None, None] * dist
    k_spec = pl.BlockSpec((1, s, hd), lambda bi, hi, qi: (bi, 0, hi))
    vt_spec = pl.BlockSpec((1, hd, s), lambda bi, hi, qi: (bi, hi, 0))
    q_spec = pl.BlockSpec((1, tq, hd), lambda bi, hi, qi: (bi, qi, hi))
    return pl.pallas_call(
        functools.partial(_attn_body, tq=tq, tk=tk, seq=s, lam_init=lam_init),
        grid=(b, ATTN_HEADS, s // tq),
        in_specs=[_const_spec(lam_params.shape),
                  _const_spec((1, hd)),
                  q_spec,
                  pl.BlockSpec((1, tq, LANES), lambda bi, hi, qi: (hi, qi, 0)),
                  pl.BlockSpec((1, n_diag, tk, tq), lambda bi, hi, qi: (hi, 0, 0, 0)),
                  k_spec,
                  pl.BlockSpec((1, s, LANES), lambda bi, hi, qi: (hi, 0, 0)),
                  vt_spec],
        out_specs=q_spec,
        out_shape=jax.ShapeDtypeStruct((b, s, ATTN_WIDTH), BF16),
        scratch_shapes=[pltpu.VMEM((hd + ONES_ROWS, tq), F32), pltpu.VMEM((hd + ONES_ROWS, tq), F32)],
        compiler_params=_params("parallel", "parallel", "parallel"),
        name="diff_attn",
    )(lam_params, subln_g.reshape(1, hd), q3, q_tab, diag_bias, k3, k_tab, vt3)


def _merge_body(x_ref, g_ref, wgate_ref, mb_ref, pool_ref, attn_ref, lru_ref,
                wbp_ref, wba_ref, wbl_ref, wout_ref, o_ref):
    x = x_ref[...]
    d = x.shape[-1]
    h = _rms(x, g_ref[...], NORM_EPS).astype(BF16)
    merged = None
    for i, (br_ref, wb_ref) in enumerate(((pool_ref, wbp_ref), (attn_ref, wba_ref), (lru_ref, wbl_ref))):
        logits = _dot(h, wgate_ref[:, i * d:(i + 1) * d]) + mb_ref[i:i + 1, :]
        term = jax.nn.sigmoid(logits) * _dot(br_ref[...], wb_ref[...])
        merged = term if merged is None else merged + term
    o_ref[...] = x + _dot(merged.astype(BF16), wout_ref[...])


def _merge(x2, g, w_gate, merge_bias, pool2, attn2, lru2, wbp, wba, wbl, w_out, *, tm=512):
    n, d = x2.shape

    def tok(width):
        return pl.BlockSpec((tm, width), lambda i: (i, 0))

    return pl.pallas_call(
        _merge_body,
        grid=(n // tm,),
        in_specs=[tok(d), _const_spec((1, d)), _const_spec(w_gate.shape), _const_spec(merge_bias.shape),
                  tok(pool2.shape[1]), tok(attn2.shape[1]), tok(lru2.shape[1]),
                  _const_spec(wbp.shape), _const_spec(wba.shape), _const_spec(wbl.shape),
                  _const_spec(w_out.shape)],
        out_specs=tok(d),
        out_shape=jax.ShapeDtypeStruct((n, d), F32),
        compiler_params=_params("parallel"),
        name="merge",
    )(x2, g.reshape(1, d), w_gate, merge_bias, pool2, attn2, lru2, wbp, wba, wbl, w_out)


def _block_diag(w):
    g, c, _ = w.shape
    eye = jnp.eye(g, dtype=w.dtype)
    return (eye[:, None, :, None] * w[:, :, None, :]).reshape(g * c, g * c)


def kernel(x, ffn1_norm, ffn1_w_in, ffn1_w_out, mix_norm, w_in, pool_w, pool_scale, attn_lambda, attn_subln, lru_conv_w, lru_conv_b, lru_w_a, lru_b_a, lru_w_x, lru_b_x, lru_lambda, w_branch_pool, w_branch_attn, w_branch_lru, merge_bias, w_out, ffn2_norm, ffn2_w_in, ffn2_w_out, final_norm):
    b, s, d = x.shape
    depth = ffn1_norm.shape[0]
    qk = ATTN_HEADS * 2 * ATTN_QK_DIM
    mix_cols = POOL_WIDTH + 2 * qk + ATTN_WIDTH + 2 * LRU_WIDTH
    x2 = x.reshape(b * s, d)
    for l in range(depth):
        lam_init = 0.8 - 0.6 * math.exp(-0.3 * l)
        last = l == depth - 1
        x2 = _ffn(x2, ffn1_norm[l], ffn1_w_in[l].astype(BF16), ffn1_w_out[l].astype(BF16))

        p, q, k, vt, lx, lg = _inproj(x2, mix_norm[l], w_in[l, :, :mix_cols].astype(BF16), s)
        pool = _pool(p.reshape(b, s, -1), _block_diag(pool_w[l]).astype(BF16), pool_scale[l])
        w_gates = jnp.concatenate(
            [_block_diag(lru_w_a[l, 0]), _block_diag(lru_w_x[l, 0]),
             _block_diag(lru_w_a[l, 1]), _block_diag(lru_w_x[l, 1])], axis=1).astype(BF16)
        b_gates = jnp.concatenate([lru_b_a[l, 0], lru_b_x[l, 0], lru_b_a[l, 1], lru_b_x[l, 1]])
        lru = _lru(lx.reshape(b, s, -1), lg.reshape(b, s, -1), lru_conv_w[l], lru_conv_b[l],
                   w_gates, b_gates, lru_lambda[l])
        attn = _attn(q.reshape(b, s, -1), k.reshape(b, s, -1), vt,
                     attn_lambda[l], attn_subln[l], lam_init)
        x2 = _merge(x2, mix_norm[l], w_in[l, :, mix_cols:].astype(BF16), merge_bias[l],
                    pool.reshape(b * s, -1), attn.reshape(b * s, -1), lru.reshape(b * s, -1),
                    w_branch_pool[l].astype(BF16), w_branch_attn[l].astype(BF16),
                    w_branch_lru[l].astype(BF16), w_out[l].astype(BF16))

        x2 = _ffn(x2, ffn2_norm[l], ffn2_w_in[l].astype(BF16), ffn2_w_out[l].astype(BF16),
                  final_norm if last else None)
    return x2.reshape(b, s, d)
```

```python
import functools
import math

import jax
import jax.numpy as jnp
from jax import lax
from jax.experimental import pallas as pl
from jax.experimental.pallas import tpu as pltpu

F32 = jnp.float32
BF16 = jnp.bfloat16

NORM_EPS = 1e-6
SUBLN_EPS = 1e-5
LRU_C = 8.0

POOL_WINDOWS = (2, 4, 8, 16)
POOL_WIDTH = 256
ATTN_HEADS = 4
ATTN_QK_DIM = 64
ATTN_V_DIM = 128
ATTN_WIDTH = ATTN_HEADS * ATTN_V_DIM
LRU_WIDTH = 256
LRU_CONV_WIDTH = 4

LOG2E = math.log2(math.e)
Q_SCALE = ATTN_QK_DIM ** -0.5 * LOG2E

LANES = 128
MXU_COLS = 256
VMEM_LIMIT_BYTES = 52 * 1024 * 1024


def _params(*sem):
    return pltpu.CompilerParams(dimension_semantics=sem,
                                vmem_limit_bytes=VMEM_LIMIT_BYTES)


def _const_spec(shape):
    nd = len(shape)
    return pl.BlockSpec(shape, lambda *_: (0,) * nd, pipeline_mode=pl.Buffered(1))


def _rms(x, g, eps):
    ms = jnp.mean(x * x, axis=-1, keepdims=True)
    return x * lax.rsqrt(ms + eps) * g


def _dot(a, b):
    return jnp.dot(a, b, preferred_element_type=F32)


def _dot_nt(a, b):
    return lax.dot_general(a, b, (((1,), (1,)), ((), ())), preferred_element_type=F32)


def _ffn_body(*refs, d_ff, chunk, final):
    if final:
        x_ref, g_ref, win_ref, wout_ref, fg_ref, o_ref, acc_ref = refs
    else:
        x_ref, g_ref, win_ref, wout_ref, o_ref, acc_ref = refs
    x = x_ref[...]
    h = _rms(x, g_ref[...], NORM_EPS).astype(BF16)
    for c in range(d_ff // chunk):
        lo = c * chunk
        gate = _dot(h, win_ref[:, lo:lo + chunk])
        up = _dot(h, win_ref[:, d_ff + lo:d_ff + lo + chunk])
        act = (gate * jax.nn.sigmoid(gate) * up).astype(BF16)
        part = _dot(act, wout_ref[lo:lo + chunk, :])
        if c == 0:
            acc_ref[...] = part
        else:
            acc_ref[...] += part
    y = x + 0.5 * acc_ref[...]
    if final:
        y = _rms(y, fg_ref[...], NORM_EPS)
    o_ref[...] = y


def _ffn(x2, g, w_in, w_out, final_g=None, *, tm=1024):
    n, d = x2.shape
    d_ff = w_out.shape[0]
    final = final_g is not None
    in_specs = [
        pl.BlockSpec((tm, d), lambda i: (i, 0)),
        _const_spec((1, d)),
        _const_spec(w_in.shape),
        _const_spec(w_out.shape),
    ]
    args = [x2, g.reshape(1, d), w_in, w_out]
    if final:
        in_specs.append(_const_spec((1, d)))
        args.append(final_g.reshape(1, d))
    return pl.pallas_call(
        functools.partial(_ffn_body, d_ff=d_ff, chunk=MXU_COLS, final=final),
        grid=(n // tm,),
        in_specs=in_specs,
        out_specs=pl.BlockSpec((tm, d), lambda i: (i, 0)),
        out_shape=jax.ShapeDtypeStruct((n, d), F32),
        scratch_shapes=[pltpu.VMEM((tm, d), F32)],
        compiler_params=_params("parallel"),
        name="ffn_final" if final else "ffn",
    )(*args)


def _inproj_body(x_ref, g_ref, w_ref, p_ref, q_ref, k_ref, vt_ref, lx_ref, lg_ref):
    h = _rms(x_ref[...], g_ref[...], NORM_EPS).astype(BF16)
    qk = ATTN_HEADS * 2 * ATTN_QK_DIM
    off = 0

    def proj(width):
        nonlocal off
        out = _dot(h, w_ref[:, off:off + width])
        off += width
        return out

    p_ref[...] = proj(POOL_WIDTH)
    q_ref[...] = (proj(qk) * Q_SCALE).astype(BF16)
    k_ref[...] = proj(qk).astype(BF16)
    vt_ref[0] = proj(ATTN_WIDTH).T.astype(BF16)
    lx_ref[...] = proj(LRU_WIDTH)
    lg_ref[...] = proj(LRU_WIDTH)


def _inproj(x2, g, w, seq, *, tm=1024):
    n, d = x2.shape
    assert seq % tm == 0
    tiles = seq // tm
    qk = ATTN_HEADS * 2 * ATTN_QK_DIM

    def tok(width, dtype):
        return pl.BlockSpec((tm, width), lambda i: (i, 0)), jax.ShapeDtypeStruct((n, width), dtype)

    vt = (pl.BlockSpec((1, ATTN_WIDTH, tm), lambda i: (i // tiles, 0, i % tiles)),
          jax.ShapeDtypeStruct((n // seq, ATTN_WIDTH, seq), BF16))
    outs = [tok(POOL_WIDTH, F32), tok(qk, BF16), tok(qk, BF16), vt, tok(LRU_WIDTH, F32), tok(LRU_WIDTH, F32)]
    return pl.pallas_call(
        _inproj_body,
        grid=(n // tm,),
        in_specs=[pl.BlockSpec((tm, d), lambda i: (i, 0)),
                  _const_spec((1, d)),
                  _const_spec(w.shape)],
        out_specs=[spec for spec, _ in outs],
        out_shape=[shape for _, shape in outs],
        compiler_params=_params("parallel"),
        name="inproj",
    )(x2, g.reshape(1, d), w)


POOL_PAD = 16


def _pool_body(p_ref, w_ref, sc_ref, o_ref, pad_ref, *, seq, rows):
    zeros = jnp.zeros((POOL_PAD, POOL_WIDTH), F32)
    pad_ref[0:POOL_PAD, :] = zeros
    pad_ref[seq + POOL_PAD:seq + 2 * POOL_PAD, :] = zeros
    pad_ref[POOL_PAD:seq + POOL_PAD, :] = p_ref[0]

    low = lax.broadcasted_iota(jnp.int32, (rows, LANES), 1) < LANES // 2
    row = lax.broadcasted_iota(jnp.int32, (rows, LANES), 0)

    def doubled(a, d):
        return a[d:, :] + a[:a.shape[0] - d, :]

    def chunk(c, carry):
        t0 = pl.multiple_of(c * rows, rows)
        xs = pad_ref[pl.ds(t0, rows + 2 * POOL_PAD), :]
        xa, xb = xs[:, :LANES], xs[:, LANES:]
        a2 = doubled(xa, 1)
        a4 = doubled(a2, 2)
        b8 = doubled(doubled(doubled(xb, 1), 2), 4)
        b16 = doubled(b8, 8)
        sum_a = jnp.where(low, a2[POOL_PAD - 1:POOL_PAD - 1 + rows], a4[POOL_PAD - 2:POOL_PAD - 2 + rows])
        sum_b = jnp.where(low, b8[POOL_PAD - 4:POOL_PAD - 4 + rows], b16[POOL_PAD - 8:POOL_PAD - 8 + rows])
        t = t0 + row

        def count(half):
            return (jnp.minimum(t + half, seq) - jnp.maximum(t - half, 0)).astype(F32)

        cnt_a = jnp.where(low, count(1), count(2))
        cnt_b = jnp.where(low, count(4), count(8))
        mix_a = sum_a / cnt_a - xa[POOL_PAD:POOL_PAD + rows]
        mix_b = sum_b / cnt_b - xb[POOL_PAD:POOL_PAD + rows]
        mixed = jnp.concatenate([mix_a, mix_b], axis=1).astype(BF16)
        y = _dot(mixed, w_ref[...]) * sc_ref[...]
        o_ref[0, pl.ds(t0, rows), :] = y.astype(BF16)
        return carry

    lax.fori_loop(0, seq // rows, chunk, 0)


def _pool(p3, w_bd, scale, *, rows=512):
    b, s, c = p3.shape
    return pl.pallas_call(
        functools.partial(_pool_body, seq=s, rows=rows),
        grid=(b,),
        in_specs=[pl.BlockSpec((1, s, c), lambda i: (i, 0, 0)),
                  _const_spec(w_bd.shape),
                  _const_spec((1, c))],
        out_specs=pl.BlockSpec((1, s, c), lambda i: (i, 0, 0)),
        out_shape=jax.ShapeDtypeStruct((b, s, c), BF16),
        scratch_shapes=[pltpu.VMEM((s + 2 * POOL_PAD, c), F32)],
        compiler_params=_params("parallel"),
        name="pool",
    )(p3, w_bd, scale.reshape(1, c))


LRU_PAD = 8
SCAN_ROWS = 8


def _softplus(x):
    return jnp.maximum(x, 0.0) + jnp.log1p(jnp.exp(-jnp.abs(x)))


def _lru_body(lx_ref, lg_ref, cw_ref, cb_ref, wg_ref, bg_ref, lam_ref, o_ref,
              pad_ref, af_ref, uf_ref, ab_ref, ub_ref, *, seq, rows):
    w = LRU_WIDTH
    zeros = jnp.zeros((LRU_PAD, w), F32)
    pad_ref[0:LRU_PAD, :] = zeros
    pad_ref[seq + LRU_PAD:seq + 2 * LRU_PAD, :] = zeros
    pad_ref[LRU_PAD:seq + LRU_PAD, :] = lx_ref[0]

    decay = -LRU_C * _softplus(-lam_ref[...])
    lead = LRU_CONV_WIDTH // 2
    row_in_blk = lax.broadcasted_iota(jnp.int32, (rows, w), 0) % SCAN_ROWS

    def blk_shift(x, d, fill, reverse):
        blocks = x.reshape(rows // SCAN_ROWS, SCAN_ROWS, w)
        if reverse:
            moved = pltpu.roll(blocks, SCAN_ROWS - d, axis=1).reshape(rows, w)
            return jnp.where(row_in_blk < SCAN_ROWS - d, moved, fill)
        moved = pltpu.roll(blocks, d, axis=1).reshape(rows, w)
        return jnp.where(row_in_blk >= d, moved, fill)

    def gates(c, carry):
        t0 = pl.multiple_of(c * rows, rows)
        xs = pad_ref[pl.ds(t0, rows + 2 * LRU_PAD), :]
        xf = cb_ref[...]
        for j in range(LRU_CONV_WIDTH):
            s0 = LRU_PAD - lead + j
            xf = xf + cw_ref[j:j + 1, :] * xs[s0:s0 + rows]
        g = _dot(xf.astype(BF16), wg_ref[...]) + bg_ref[...]
        for d, (a_ref, u_ref) in enumerate(((af_ref, uf_ref), (ab_ref, ub_ref))):
            r = jax.nn.sigmoid(g[:, (2 * d) * w:(2 * d + 1) * w])
            i = jax.nn.sigmoid(g[:, (2 * d + 1) * w:(2 * d + 2) * w])
            log_a = decay[d:d + 1, :] * r
            a = jnp.exp(log_a)
            one_minus_a2 = -jnp.tanh(log_a) * (a * a + 1.0)
            u = jnp.sqrt(one_minus_a2) * (i * xf)
            step = 1
            while step < SCAN_ROWS:
                u = u + a * blk_shift(u, step, 0.0, d == 1)
                a = a * blk_shift(a, step, 1.0, d == 1)
                step *= 2
            a_ref[pl.ds(t0, rows), :] = a
            u_ref[pl.ds(t0, rows), :] = u
        return carry

    lax.fori_loop(0, seq // rows, gates, 0)

    def carry_blocks(i, carry):
        hf, hb = carry
        base_f = pl.multiple_of(i * SCAN_ROWS, SCAN_ROWS)
        base_b = pl.multiple_of(seq - SCAN_ROWS - i * SCAN_ROWS, SCAN_ROWS)
        tf = uf_ref[pl.ds(base_f, SCAN_ROWS), :] + af_ref[pl.ds(base_f, SCAN_ROWS), :] * hf
        uf_ref[pl.ds(base_f, SCAN_ROWS), :] = tf
        tb = ub_ref[pl.ds(base_b, SCAN_ROWS), :] + ab_ref[pl.ds(base_b, SCAN_ROWS), :] * hb
        ub_ref[pl.ds(base_b, SCAN_ROWS), :] = tb
        return tf[SCAN_ROWS - 1:SCAN_ROWS, :], tb[0:1, :]

    h0 = jnp.zeros((1, w), F32)
    lax.fori_loop(0, seq // SCAN_ROWS, carry_blocks, (h0, h0), unroll=8)

    def gate_out(c, carry):
        t0 = pl.multiple_of(c * rows, rows)
        h = uf_ref[pl.ds(t0, rows), :] + ub_ref[pl.ds(t0, rows), :]
        y = jax.nn.gelu(lg_ref[0, pl.ds(t0, rows), :]) * h
        o_ref[0, pl.ds(t0, rows), :] = y.astype(BF16)
        return carry

    lax.fori_loop(0, seq // rows, gate_out, 0)


def _lru(lx3, lg3, conv_w, conv_b, w_gates, b_gates, lam, *, rows=512):
    b, s, c = lx3.shape
    seq_spec = pl.BlockSpec((1, s, c), lambda i: (i, 0, 0))
    return pl.pallas_call(
        functools.partial(_lru_body, seq=s, rows=rows),
        grid=(b,),
        in_specs=[seq_spec, seq_spec,
                  _const_spec(conv_w.shape), _const_spec((1, c)),
                  _const_spec(w_gates.shape), _const_spec((1, 4 * c)),
                  _const_spec(lam.shape)],
        out_specs=seq_spec,
        out_shape=jax.ShapeDtypeStruct((b, s, c), BF16),
        scratch_shapes=[pltpu.VMEM((s + 2 * LRU_PAD, c), F32)]
        + [pltpu.VMEM((s, c), F32) for _ in range(4)],
        compiler_params=_params("parallel"),
        name="lru",
    )(lx3, lg3, conv_w, conv_b.reshape(1, c), w_gates, b_gates.reshape(1, 4 * c), lam)


ALIBI_SPLIT = 64
ONES_ROWS = 16
SKEW = 2
SCORE_LIMIT = 32.0
NORM_MARGIN = 0.98
VALUE_LIMIT = 2.0 ** 80

def _bf16_split3(c):
    pieces = []
    for _ in range(3):
        p = c.astype(BF16).astype(F32)
        pieces.append(p)
        c = c - p
    return pieces


def _alibi_tables(seq):
    assert seq <= ALIBI_SPLIT * 256
    slopes = jnp.asarray([2.0 ** (-8.0 / ATTN_HEADS * (i + 1)) for i in range(ATTN_HEADS)], F32)
    coef = slopes * LOG2E
    pos = jnp.arange(seq)
    parts = [jnp.broadcast_to((ALIBI_SPLIT * (pos // ALIBI_SPLIT)).astype(F32), (ATTN_HEADS, seq)),
             jnp.broadcast_to((pos % ALIBI_SPLIT).astype(F32), (ATTN_HEADS, seq))]
    cs = [jnp.broadcast_to(c[:, None], (ATTN_HEADS, seq)) for c in _bf16_split3(coef)]
    q_cols = [part for part in parts for _ in cs] + [c for _ in parts for c in cs]
    k_cols = [-c for _ in parts for c in cs] + [part for part in parts for _ in cs]
    pad = ((0, 0), (0, 0), (0, LANES - len(q_cols)))
    q_tab = jnp.pad(jnp.stack(q_cols, axis=-1), pad).astype(BF16)
    k_tab = jnp.pad(jnp.stack(k_cols, axis=-1), pad).astype(BF16)
    return coef, q_tab, k_tab


def _attn_body(lam_ref, g_ref, q_ref, qtab_ref, dbias_ref, k_ref, ktab_ref, vt_ref, o_ref,
               acc1_ref, acc2_ref, stat_ref, *, tq, tk, seq, lam_init):
    n_diag = tq // tk
    diag = pl.program_id(2) * n_diag

    qf = q_ref[0].astype(F32)
    lane = lax.broadcasted_iota(jnp.int32, qf.shape, 1)
    q1 = jnp.where(lane < ATTN_QK_DIM, qf, 0.0).astype(BF16)
    q2 = jnp.where(lane >= ATTN_QK_DIM, qf, 0.0).astype(BF16)
    tab_before = qtab_ref[0]
    tab_after = (-tab_before.astype(F32)).astype(BF16)

    same_map = ((lax.broadcasted_iota(jnp.int32, (LANES, LANES), 0) < ATTN_QK_DIM)
                == (lax.broadcasted_iota(jnp.int32, (LANES, LANES), 1) < ATTN_QK_DIM))
    map_ones = jnp.where(same_map, 1.0, 0.0).astype(BF16)

    def max_sq_norm(x):
        return jnp.max(_dot((x * x).astype(BF16), map_ones), axis=0, keepdims=True)

    @pl.when(pl.program_id(2) == 0)
    def _():
        stat_ref[0:1, :] = max_sq_norm(k_ref[0].astype(F32))
        stat_ref[1:2, :] = jnp.full((1, LANES), jnp.max(jnp.abs(vt_ref[0].astype(F32))), F32)

    excess = jnp.maximum(max_sq_norm(qf) * stat_ref[0:1, :] - SCORE_LIMIT ** 2 * NORM_MARGIN,
                         stat_ref[1:2, :] - VALUE_LIMIT)
    plain_softmax_ok = jnp.max(excess) <= 0.0

    acc1_ref[...] = jnp.zeros_like(acc1_ref)
    acc2_ref[...] = jnp.zeros_like(acc2_ref)

    hd = ATTN_V_DIM
    ones = jnp.ones((ONES_ROWS, tk), BF16)

    def values(k0):
        return jnp.concatenate([vt_ref[0, :, pl.ds(k0, tk)], ones], axis=0)

    def update(s, m, acc_ref, k0):
        m_new = jnp.maximum(m, jnp.max(s, axis=0, keepdims=True))
        p = jnp.exp2(s - m_new)
        acc_ref[...] = jnp.exp2(m - m_new) * acc_ref[...] + _dot(values(k0), p.astype(BF16))
        return m_new

    def update_plain(s, m, acc_ref, k0):
        acc_ref[...] += _dot(values(k0), jnp.exp2(s).astype(BF16))
        return m

    def folded_scores(jj):
        j = jj + n_diag * (jj >= diag).astype(jnp.int32)
        tab = jnp.where(j > diag, tab_after, tab_before)
        k0 = pl.multiple_of(j * tk, tk)
        ka = jnp.concatenate([k_ref[0, pl.ds(k0, tk), :], ktab_ref[0, pl.ds(k0, tk), :]], axis=1)
        return (_dot_nt(ka, jnp.concatenate([q1, tab], axis=1)),
                _dot_nt(ka, jnp.concatenate([q2, tab], axis=1)), k0)

    def diag_scores(r):
        k0 = pl.multiple_of((diag + r) * tk, tk)
        kc = k_ref[0, pl.ds(k0, tk), :]
        return _dot_nt(kc, q1) + dbias_ref[0, r], _dot_nt(kc, q2) + dbias_ref[0, r], k0

    def all_chunks(step_fn):
        m1 = m2 = jnp.full((1, tq), -jnp.inf, F32)
        pending = []
        for step in range(seq // tk):
            pending.append(diag_scores(step) if step < n_diag else folded_scores(jnp.int32(step - n_diag)))
            if len(pending) > SKEW:
                s1, s2, k0 = pending.pop(0)
                m1 = step_fn(s1, m1, acc1_ref, k0)
                m2 = step_fn(s2, m2, acc2_ref, k0)
        for s1, s2, k0 in pending:
            m1 = step_fn(s1, m1, acc1_ref, k0)
            m2 = step_fn(s2, m2, acc2_ref, k0)

    @pl.when(plain_softmax_ok)
    def _():
        all_chunks(update_plain)

    @pl.when(jnp.logical_not(plain_softmax_ok))
    def _():
        all_chunks(update)

    lp = lam_ref[...]
    lam = (jnp.exp(jnp.sum(lp[0:1] * lp[1:2], axis=-1, keepdims=True))
           - jnp.exp(jnp.sum(lp[2:3] * lp[3:4], axis=-1, keepdims=True)) + lam_init)
    inv1 = 1.0 / acc1_ref[hd:hd + 1, :]
    inv2 = lam / acc2_ref[hd:hd + 1, :]
    ot = acc1_ref[:hd, :] * inv1 - acc2_ref[:hd, :] * inv2
    ot = ot * lax.rsqrt(jnp.mean(ot * ot, axis=0, keepdims=True) + SUBLN_EPS)
    o_ref[0] = (ot.T * g_ref[...] * (1.0 - lam_init)).astype(BF16)


def _attn(q3, k3, vt3, lam_params, subln_g, lam_init, *, tq=1024, tk=512):
    b, s, _ = q3.shape
    hd = ATTN_V_DIM
    assert tq % tk == 0 and s % tq == 0
    n_diag = tq // tk
    coef, q_tab, k_tab = _alibi_tables(s)
    key_pos = (jnp.arange(n_diag)[:, None, None] * tk + jnp.arange(tk)[None, :, None])
    dist = jnp.abs(key_pos - jnp.arange(tq)[None, None, :]).astype(F32)
    diag_bias = -coef[:, None, None, None] * dist
    k_spec = pl.BlockSpec((1, s, hd), lambda bi, hi, qi: (bi, 0, hi))
    vt_spec = pl.BlockSpec((1, hd, s), lambda bi, hi, qi: (bi, hi, 0))
    q_spec = pl.BlockSpec((1, tq, hd), lambda bi, hi, qi: (bi, qi, hi))
    return pl.pallas_call(
        functools.partial(_attn_body, tq=tq, tk=tk, seq=s, lam_init=lam_init),
        grid=(b, ATTN_HEADS, s // tq),
        in_specs=[_const_spec(lam_params.shape),
                  _const_spec((1, hd)),
                  q_spec,
                  pl.BlockSpec((1, tq, LANES), lambda bi, hi, qi: (hi, qi, 0)),
                  pl.BlockSpec((1, n_diag, tk, tq), lambda bi, hi, qi: (hi, 0, 0, 0)),
                  k_spec,
                  pl.BlockSpec((1, s, LANES), lambda bi, hi, qi: (hi, 0, 0)),
                  vt_spec],
        out_specs=q_spec,
        out_shape=jax.ShapeDtypeStruct((b, s, ATTN_WIDTH), BF16),
        scratch_shapes=[pltpu.VMEM((hd + ONES_ROWS, tq), F32), pltpu.VMEM((hd + ONES_ROWS, tq), F32),
                        pltpu.VMEM((8, LANES), F32)],
        compiler_params=_params("parallel", "parallel", "arbitrary"),
        name="diff_attn",
    )(lam_params, subln_g.reshape(1, hd), q3, q_tab, diag_bias, k3, k_tab, vt3)


def _merge_body(x_ref, g_ref, wgate_ref, mb_ref, pool_ref, attn_ref, lru_ref,
                wbp_ref, wba_ref, wbl_ref, wout_ref, o_ref):
    x = x_ref[...]
    d = x.shape[-1]
    h = _rms(x, g_ref[...], NORM_EPS).astype(BF16)
    merged = None
    for i, (br_ref, wb_ref) in enumerate(((pool_ref, wbp_ref), (attn_ref, wba_ref), (lru_ref, wbl_ref))):
        logits = _dot(h, wgate_ref[:, i * d:(i + 1) * d]) + mb_ref[i:i + 1, :]
        term = jax.nn.sigmoid(logits) * _dot(br_ref[...], wb_ref[...])
        merged = term if merged is None else merged + term
    o_ref[...] = x + _dot(merged.astype(BF16), wout_ref[...])


def _merge(x2, g, w_gate, merge_bias, pool2, attn2, lru2, wbp, wba, wbl, w_out, *, tm=1024):
    n, d = x2.shape

    def tok(width):
        return pl.BlockSpec((tm, width), lambda i: (i, 0))

    return pl.pallas_call(
        _merge_body,
        grid=(n // tm,),
        in_specs=[tok(d), _const_spec((1, d)), _const_spec(w_gate.shape), _const_spec(merge_bias.shape),
                  tok(pool2.shape[1]), tok(attn2.shape[1]), tok(lru2.shape[1]),
                  _const_spec(wbp.shape), _const_spec(wba.shape), _const_spec(wbl.shape),
                  _const_spec(w_out.shape)],
        out_specs=tok(d),
        out_shape=jax.ShapeDtypeStruct((n, d), F32),
        compiler_params=_params("parallel"),
        name="merge",
    )(x2, g.reshape(1, d), w_gate, merge_bias, pool2, attn2, lru2, wbp, wba, wbl, w_out)


def _block_diag(w):
    g, c, _ = w.shape
    eye = jnp.eye(g, dtype=w.dtype)
    return (eye[:, None, :, None] * w[:, :, None, :]).reshape(g * c, g * c)


def kernel(x, ffn1_norm, ffn1_w_in, ffn1_w_out, mix_norm, w_in, pool_w, pool_scale, attn_lambda, attn_subln, lru_conv_w, lru_conv_b, lru_w_a, lru_b_a, lru_w_x, lru_b_x, lru_lambda, w_branch_pool, w_branch_attn, w_branch_lru, merge_bias, w_out, ffn2_norm, ffn2_w_in, ffn2_w_out, final_norm):
    b, s, d = x.shape
    depth = ffn1_norm.shape[0]
    qk = ATTN_HEADS * 2 * ATTN_QK_DIM
    mix_cols = POOL_WIDTH + 2 * qk + ATTN_WIDTH + 2 * LRU_WIDTH
    x2 = x.reshape(b * s, d)
    for l in range(depth):
        lam_init = 0.8 - 0.6 * math.exp(-0.3 * l)
        last = l == depth - 1
        x2 = _ffn(x2, ffn1_norm[l], ffn1_w_in[l].astype(BF16), ffn1_w_out[l].astype(BF16))

        p, q, k, vt, lx, lg = _inproj(x2, mix_norm[l], w_in[l, :, :mix_cols].astype(BF16), s)
        pool = _pool(p.reshape(b, s, -1), _block_diag(pool_w[l]).astype(BF16), pool_scale[l])
        w_gates = jnp.concatenate(
            [_block_diag(lru_w_a[l, 0]), _block_diag(lru_w_x[l, 0]),
             _block_diag(lru_w_a[l, 1]), _block_diag(lru_w_x[l, 1])], axis=1).astype(BF16)
        b_gates = jnp.concatenate([lru_b_a[l, 0], lru_b_x[l, 0], lru_b_a[l, 1], lru_b_x[l, 1]])
        lru = _lru(lx.reshape(b, s, -1), lg.reshape(b, s, -1), lru_conv_w[l], lru_conv_b[l],
                   w_gates, b_gates, lru_lambda[l])
        attn = _attn(q.reshape(b, s, -1), k.reshape(b, s, -1), vt,
                     attn_lambda[l], attn_subln[l], lam_init)
        x2 = _merge(x2, mix_norm[l], w_in[l, :, mix_cols:].astype(BF16), merge_bias[l],
                    pool.reshape(b * s, -1), attn.reshape(b * s, -1), lru.reshape(b * s, -1),
                    w_branch_pool[l].astype(BF16), w_branch_attn[l].astype(BF16),
                    w_branch_lru[l].astype(BF16), w_out[l].astype(BF16))

        x2 = _ffn(x2, ffn2_norm[l], ffn2_w_in[l].astype(BF16), ffn2_w_out[l].astype(BF16),
                  final_norm if last else None)
    return x2.reshape(b, s, d)
```

```python
import functools
import math

import jax
import jax.numpy as jnp
from jax import lax
from jax.experimental import pallas as pl
from jax.experimental.pallas import tpu as pltpu

F32 = jnp.float32
BF16 = jnp.bfloat16

NORM_EPS = 1e-6
SUBLN_EPS = 1e-5
LRU_C = 8.0

POOL_WINDOWS = (2, 4, 8, 16)
POOL_WIDTH = 256
ATTN_HEADS = 4
ATTN_QK_DIM = 64
ATTN_V_DIM = 128
ATTN_WIDTH = ATTN_HEADS * ATTN_V_DIM
LRU_WIDTH = 256
LRU_CONV_WIDTH = 4

LOG2E = math.log2(math.e)
Q_SCALE = ATTN_QK_DIM ** -0.5 * LOG2E

LANES = 128
SUBLANES = 8
MXU_COLS = 256
VMEM_LIMIT_BYTES = 52 * 1024 * 1024


def _params(*sem):
    return pltpu.CompilerParams(dimension_semantics=sem,
                                vmem_limit_bytes=VMEM_LIMIT_BYTES)


def _const_spec(shape):
    nd = len(shape)
    return pl.BlockSpec(shape, lambda *_: (0,) * nd, pipeline_mode=pl.Buffered(1))


def _rms(x, g, eps):
    ms = jnp.mean(x * x, axis=-1, keepdims=True)
    return x * lax.rsqrt(ms + eps) * g


def _dot(a, b):
    return jnp.dot(a, b, preferred_element_type=F32)


def _dot_nt(a, b):
    return lax.dot_general(a, b, (((1,), (1,)), ((), ())), preferred_element_type=F32)


def _ffn_body(*refs, d_ff, chunk, final):
    if final:
        x_ref, g_ref, win_ref, wout_ref, fg_ref, o_ref, acc_ref = refs
    else:
        x_ref, g_ref, win_ref, wout_ref, o_ref, acc_ref = refs
    x = x_ref[...]
    h = _rms(x, g_ref[...], NORM_EPS).astype(BF16)
    for c in range(d_ff // chunk):
        lo = c * chunk
        gate = _dot(h, win_ref[:, lo:lo + chunk])
        up = _dot(h, win_ref[:, d_ff + lo:d_ff + lo + chunk])
        act = (gate * jax.nn.sigmoid(gate) * up).astype(BF16)
        part = _dot(act, wout_ref[lo:lo + chunk, :])
        if c == 0:
            acc_ref[...] = part
        else:
            acc_ref[...] += part
    y = x + 0.5 * acc_ref[...]
    if final:
        y = _rms(y, fg_ref[...], NORM_EPS)
    o_ref[...] = y


def _ffn(x2, g, w_in, w_out, final_g=None, *, tm=1024):
    n, d = x2.shape
    d_ff = w_out.shape[0]
    final = final_g is not None
    in_specs = [
        pl.BlockSpec((tm, d), lambda i: (i, 0)),
        _const_spec((1, d)),
        _const_spec(w_in.shape),
        _const_spec(w_out.shape),
    ]
    args = [x2, g.reshape(1, d), w_in, w_out]
    if final:
        in_specs.append(_const_spec((1, d)))
        args.append(final_g.reshape(1, d))
    return pl.pallas_call(
        functools.partial(_ffn_body, d_ff=d_ff, chunk=MXU_COLS, final=final),
        grid=(n // tm,),
        in_specs=in_specs,
        out_specs=pl.BlockSpec((tm, d), lambda i: (i, 0)),
        out_shape=jax.ShapeDtypeStruct((n, d), F32),
        scratch_shapes=[pltpu.VMEM((tm, d), F32)],
        compiler_params=_params("parallel"),
        name="ffn_final" if final else "ffn",
    )(*args)


def _inproj_body(x_ref, g_ref, w_ref, p_ref, q_ref, k_ref, vt_ref, lx_ref, lg_ref):
    h = _rms(x_ref[...], g_ref[...], NORM_EPS).astype(BF16)
    qk = ATTN_HEADS * 2 * ATTN_QK_DIM
    off = 0

    def proj(width):
        nonlocal off
        out = _dot(h, w_ref[:, off:off + width])
        off += width
        return out

    p_ref[...] = proj(POOL_WIDTH)
    q_ref[...] = (proj(qk) * Q_SCALE).astype(BF16)
    k_ref[...] = proj(qk).astype(BF16)
    vt_ref[0] = proj(ATTN_WIDTH).T.astype(BF16)
    lx_ref[...] = proj(LRU_WIDTH)
    lg_ref[...] = proj(LRU_WIDTH)


def _inproj(x2, g, w, seq, *, tm=1024):
    n, d = x2.shape
    assert seq % tm == 0
    tiles = seq // tm
    qk = ATTN_HEADS * 2 * ATTN_QK_DIM

    def tok(width, dtype):
        return pl.BlockSpec((tm, width), lambda i: (i, 0)), jax.ShapeDtypeStruct((n, width), dtype)

    vt = (pl.BlockSpec((1, ATTN_WIDTH, tm), lambda i: (i // tiles, 0, i % tiles)),
          jax.ShapeDtypeStruct((n // seq, ATTN_WIDTH, seq), BF16))
    outs = [tok(POOL_WIDTH, F32), tok(qk, BF16), tok(qk, BF16), vt, tok(LRU_WIDTH, F32), tok(LRU_WIDTH, F32)]
    return pl.pallas_call(
        _inproj_body,
        grid=(n // tm,),
        in_specs=[pl.BlockSpec((tm, d), lambda i: (i, 0)),
                  _const_spec((1, d)),
                  _const_spec(w.shape)],
        out_specs=[spec for spec, _ in outs],
        out_shape=[shape for _, shape in outs],
        compiler_params=_params("parallel"),
        name="inproj",
    )(x2, g.reshape(1, d), w)


POOL_PAD = 16


def _pool_body(p_ref, w_ref, sc_ref, o_ref, pad_ref, *, seq, rows):
    zeros = jnp.zeros((POOL_PAD, POOL_WIDTH), F32)
    pad_ref[0:POOL_PAD, :] = zeros
    pad_ref[seq + POOL_PAD:seq + 2 * POOL_PAD, :] = zeros
    pad_ref[POOL_PAD:seq + POOL_PAD, :] = p_ref[0]

    low = lax.broadcasted_iota(jnp.int32, (rows, LANES), 1) < LANES // 2
    row = lax.broadcasted_iota(jnp.int32, (rows, LANES), 0)

    def doubled(a, d):
        return a[d:, :] + a[:a.shape[0] - d, :]

    def chunk(c, carry):
        t0 = pl.multiple_of(c * rows, rows)
        xs = pad_ref[pl.ds(t0, rows + 2 * POOL_PAD), :]
        xa, xb = xs[:, :LANES], xs[:, LANES:]
        a2 = doubled(xa, 1)
        a4 = doubled(a2, 2)
        b8 = doubled(doubled(doubled(xb, 1), 2), 4)
        b16 = doubled(b8, 8)
        sum_a = jnp.where(low, a2[POOL_PAD - 1:POOL_PAD - 1 + rows], a4[POOL_PAD - 2:POOL_PAD - 2 + rows])
        sum_b = jnp.where(low, b8[POOL_PAD - 4:POOL_PAD - 4 + rows], b16[POOL_PAD - 8:POOL_PAD - 8 + rows])
        t = t0 + row

        def count(half):
            return (jnp.minimum(t + half, seq) - jnp.maximum(t - half, 0)).astype(F32)

        cnt_a = jnp.where(low, count(1), count(2))
        cnt_b = jnp.where(low, count(4), count(8))
        mix_a = sum_a / cnt_a - xa[POOL_PAD:POOL_PAD + rows]
        mix_b = sum_b / cnt_b - xb[POOL_PAD:POOL_PAD + rows]
        mixed = jnp.concatenate([mix_a, mix_b], axis=1).astype(BF16)
        y = _dot(mixed, w_ref[...]) * sc_ref[...]
        o_ref[0, pl.ds(t0, rows), :] = y.astype(BF16)
        return carry

    lax.fori_loop(0, seq // rows, chunk, 0)


def _pool(p3, w_bd, scale, *, rows=512):
    b, s, c = p3.shape
    return pl.pallas_call(
        functools.partial(_pool_body, seq=s, rows=rows),
        grid=(b,),
        in_specs=[pl.BlockSpec((1, s, c), lambda i: (i, 0, 0)),
                  _const_spec(w_bd.shape),
                  _const_spec((1, c))],
        out_specs=pl.BlockSpec((1, s, c), lambda i: (i, 0, 0)),
        out_shape=jax.ShapeDtypeStruct((b, s, c), BF16),
        scratch_shapes=[pltpu.VMEM((s + 2 * POOL_PAD, c), F32)],
        compiler_params=_params("parallel"),
        name="pool",
    )(p3, w_bd, scale.reshape(1, c))


LRU_PAD = 8
SCAN_ROWS = 8


def _softplus(x):
    return jnp.maximum(x, 0.0) + jnp.log1p(jnp.exp(-jnp.abs(x)))


def _lru_body(lx_ref, lg_ref, cw_ref, cb_ref, wg_ref, bg_ref, lam_ref, o_ref,
              pad_ref, af_ref, uf_ref, ab_ref, ub_ref, *, seq, rows):
    w = LRU_WIDTH
    zeros = jnp.zeros((LRU_PAD, w), F32)
    pad_ref[0:LRU_PAD, :] = zeros
    pad_ref[seq + LRU_PAD:seq + 2 * LRU_PAD, :] = zeros
    pad_ref[LRU_PAD:seq + LRU_PAD, :] = lx_ref[0]

    decay = -LRU_C * _softplus(-lam_ref[...])
    lead = LRU_CONV_WIDTH // 2
    row_in_blk = lax.broadcasted_iota(jnp.int32, (rows, w), 0) % SCAN_ROWS

    def blk_shift(x, d, fill, reverse):
        blocks = x.reshape(rows // SCAN_ROWS, SCAN_ROWS, w)
        if reverse:
            moved = pltpu.roll(blocks, SCAN_ROWS - d, axis=1).reshape(rows, w)
            return jnp.where(row_in_blk < SCAN_ROWS - d, moved, fill)
        moved = pltpu.roll(blocks, d, axis=1).reshape(rows, w)
        return jnp.where(row_in_blk >= d, moved, fill)

    def gates(c, carry):
        t0 = pl.multiple_of(c * rows, rows)
        xs = pad_ref[pl.ds(t0, rows + 2 * LRU_PAD), :]
        xf = cb_ref[...]
        for j in range(LRU_CONV_WIDTH):
            s0 = LRU_PAD - lead + j
            xf = xf + cw_ref[j:j + 1, :] * xs[s0:s0 + rows]
        g = _dot(xf.astype(BF16), wg_ref[...]) + bg_ref[...]
        for d, (a_ref, u_ref) in enumerate(((af_ref, uf_ref), (ab_ref, ub_ref))):
            r = jax.nn.sigmoid(g[:, (2 * d) * w:(2 * d + 1) * w])
            i = jax.nn.sigmoid(g[:, (2 * d + 1) * w:(2 * d + 2) * w])
            log_a = decay[d:d + 1, :] * r
            a = jnp.exp(log_a)
            one_minus_a2 = -jnp.tanh(log_a) * (a * a + 1.0)
            u = jnp.sqrt(one_minus_a2) * (i * xf)
            step = 1
            while step < SCAN_ROWS:
                u = u + a * blk_shift(u, step, 0.0, d == 1)
                a = a * blk_shift(a, step, 1.0, d == 1)
                step *= 2
            a_ref[pl.ds(t0, rows), :] = a
            u_ref[pl.ds(t0, rows), :] = u
        return carry

    lax.fori_loop(0, seq // rows, gates, 0)

    def carry_blocks(i, carry):
        hf, hb = carry
        base_f = pl.multiple_of(i * SCAN_ROWS, SCAN_ROWS)
        base_b = pl.multiple_of(seq - SCAN_ROWS - i * SCAN_ROWS, SCAN_ROWS)
        tf = uf_ref[pl.ds(base_f, SCAN_ROWS), :] + af_ref[pl.ds(base_f, SCAN_ROWS), :] * hf
        uf_ref[pl.ds(base_f, SCAN_ROWS), :] = tf
        tb = ub_ref[pl.ds(base_b, SCAN_ROWS), :] + ab_ref[pl.ds(base_b, SCAN_ROWS), :] * hb
        ub_ref[pl.ds(base_b, SCAN_ROWS), :] = tb
        return tf[SCAN_ROWS - 1:SCAN_ROWS, :], tb[0:1, :]

    h0 = jnp.zeros((1, w), F32)
    lax.fori_loop(0, seq // SCAN_ROWS, carry_blocks, (h0, h0), unroll=8)

    def gate_out(c, carry):
        t0 = pl.multiple_of(c * rows, rows)
        h = uf_ref[pl.ds(t0, rows), :] + ub_ref[pl.ds(t0, rows), :]
        y = jax.nn.gelu(lg_ref[0, pl.ds(t0, rows), :]) * h
        o_ref[0, pl.ds(t0, rows), :] = y.astype(BF16)
        return carry

    lax.fori_loop(0, seq // rows, gate_out, 0)


def _lru(lx3, lg3, conv_w, conv_b, w_gates, b_gates, lam, *, rows=512):
    b, s, c = lx3.shape
    seq_spec = pl.BlockSpec((1, s, c), lambda i: (i, 0, 0))
    return pl.pallas_call(
        functools.partial(_lru_body, seq=s, rows=rows),
        grid=(b,),
        in_specs=[seq_spec, seq_spec,
                  _const_spec(conv_w.shape), _const_spec((1, c)),
                  _const_spec(w_gates.shape), _const_spec((1, 4 * c)),
                  _const_spec(lam.shape)],
        out_specs=seq_spec,
        out_shape=jax.ShapeDtypeStruct((b, s, c), BF16),
        scratch_shapes=[pltpu.VMEM((s + 2 * LRU_PAD, c), F32)]
        + [pltpu.VMEM((s, c), F32) for _ in range(4)],
        compiler_params=_params("parallel"),
        name="lru",
    )(lx3, lg3, conv_w, conv_b.reshape(1, c), w_gates, b_gates.reshape(1, 4 * c), lam)


ALIBI_SPLIT = 64
SKEW = 2
SCORE_LIMIT = 32.0
NORM_MARGIN = 0.98
VALUE_LIMIT = 2.0 ** 80

def _bf16_split3(c):
    pieces = []
    for _ in range(3):
        p = c.astype(BF16).astype(F32)
        pieces.append(p)
        c = c - p
    return pieces


def _alibi_tables(seq):
    assert seq <= ALIBI_SPLIT * 256
    slopes = jnp.asarray([2.0 ** (-8.0 / ATTN_HEADS * (i + 1)) for i in range(ATTN_HEADS)], F32)
    coef = slopes * LOG2E
    pos = jnp.arange(seq)
    parts = [jnp.broadcast_to((ALIBI_SPLIT * (pos // ALIBI_SPLIT)).astype(F32), (ATTN_HEADS, seq)),
             jnp.broadcast_to((pos % ALIBI_SPLIT).astype(F32), (ATTN_HEADS, seq))]
    cs = [jnp.broadcast_to(c[:, None], (ATTN_HEADS, seq)) for c in _bf16_split3(coef)]
    q_cols = [part for part in parts for _ in cs] + [c for _ in parts for c in cs]
    k_cols = [-c for _ in parts for c in cs] + [part for part in parts for _ in cs]
    pad = ((0, 0), (0, 0), (0, LANES - len(q_cols)))
    q_tab = jnp.pad(jnp.stack(q_cols, axis=-1), pad).astype(BF16)
    k_tab = jnp.pad(jnp.stack(k_cols, axis=-1), pad).astype(BF16)
    return coef, q_tab, k_tab


def _attn_body(lam_ref, g_ref, q_ref, qtab_ref, dbias_ref, k_ref, ktab_ref, vt_ref, o_ref,
               acc1_ref, acc2_ref, stat_ref, *, tq, tk, seq, lam_init):
    n_diag = tq // tk
    diag = pl.program_id(2) * n_diag

    qf = q_ref[0].astype(F32)
    lane = lax.broadcasted_iota(jnp.int32, qf.shape, 1)
    q1 = jnp.where(lane < ATTN_QK_DIM, qf, 0.0).astype(BF16)
    q2 = jnp.where(lane >= ATTN_QK_DIM, qf, 0.0).astype(BF16)
    tab_before = qtab_ref[0]
    tab_after = (-tab_before.astype(F32)).astype(BF16)

    same_map = ((lax.broadcasted_iota(jnp.int32, (LANES, LANES), 0) < ATTN_QK_DIM)
                == (lax.broadcasted_iota(jnp.int32, (LANES, LANES), 1) < ATTN_QK_DIM))
    map_ones = jnp.where(same_map, 1.0, 0.0).astype(BF16)

    def max_sq_norm(x):
        return jnp.max(_dot((x * x).astype(BF16), map_ones), axis=0, keepdims=True)

    @pl.when(pl.program_id(2) == 0)
    def _():
        stat_ref[0:1, :] = max_sq_norm(k_ref[0].astype(F32))
        stat_ref[1:2, :] = jnp.full((1, LANES), jnp.max(jnp.abs(vt_ref[0].astype(F32))), F32)

    excess = jnp.maximum(max_sq_norm(qf) * stat_ref[0:1, :] - SCORE_LIMIT ** 2 * NORM_MARGIN,
                         stat_ref[1:2, :] - VALUE_LIMIT)
    plain_softmax_ok = jnp.max(excess) <= 0.0

    acc1_ref[...] = jnp.zeros_like(acc1_ref)
    acc2_ref[...] = jnp.zeros_like(acc2_ref)

    hd = ATTN_V_DIM

    def weighted(p, k0):
        pv = _dot(vt_ref[0, :, pl.ds(k0, tk)], p.astype(BF16))
        return jnp.concatenate([pv, jnp.sum(p.reshape(tk // SUBLANES, SUBLANES, tq), axis=0)], axis=0)

    def update(s, m, acc_ref, k0):
        m_new = jnp.maximum(m, jnp.max(s, axis=0, keepdims=True))
        acc_ref[...] = jnp.exp2(m - m_new) * acc_ref[...] + weighted(jnp.exp2(s - m_new), k0)
        return m_new

    def update_plain(s, m, acc_ref, k0):
        acc_ref[...] += weighted(jnp.exp2(s), k0)
        return m

    def folded_scores(jj):
        j = jj + n_diag * (jj >= diag).astype(jnp.int32)
        tab = jnp.where(j > diag, tab_after, tab_before)
        k0 = pl.multiple_of(j * tk, tk)
        ka = jnp.concatenate([k_ref[0, pl.ds(k0, tk), :], ktab_ref[0, pl.ds(k0, tk), :]], axis=1)
        return (_dot_nt(ka, jnp.concatenate([q1, tab], axis=1)),
                _dot_nt(ka, jnp.concatenate([q2, tab], axis=1)), k0)

    def diag_scores(r):
        k0 = pl.multiple_of((diag + r) * tk, tk)
        kc = k_ref[0, pl.ds(k0, tk), :]
        return _dot_nt(kc, q1) + dbias_ref[0, r], _dot_nt(kc, q2) + dbias_ref[0, r], k0

    def all_chunks(step_fn):
        m1 = m2 = jnp.full((1, tq), -jnp.inf, F32)
        pending = []
        for step in range(seq // tk):
            pending.append(diag_scores(step) if step < n_diag else folded_scores(jnp.int32(step - n_diag)))
            if len(pending) > SKEW:
                s1, s2, k0 = pending.pop(0)
                m1 = step_fn(s1, m1, acc1_ref, k0)
                m2 = step_fn(s2, m2, acc2_ref, k0)
        for s1, s2, k0 in pending:
            m1 = step_fn(s1, m1, acc1_ref, k0)
            m2 = step_fn(s2, m2, acc2_ref, k0)

    @pl.when(plain_softmax_ok)
    def _():
        all_chunks(update_plain)

    @pl.when(jnp.logical_not(plain_softmax_ok))
    def _():
        all_chunks(update)

    lp = lam_ref[...]
    lam = (jnp.exp(jnp.sum(lp[0:1] * lp[1:2], axis=-1, keepdims=True))
           - jnp.exp(jnp.sum(lp[2:3] * lp[3:4], axis=-1, keepdims=True)) + lam_init)
    inv1 = 1.0 / jnp.sum(acc1_ref[hd:, :], axis=0, keepdims=True)
    inv2 = lam / jnp.sum(acc2_ref[hd:, :], axis=0, keepdims=True)
    ot = acc1_ref[:hd, :] * inv1 - acc2_ref[:hd, :] * inv2
    ot = ot * lax.rsqrt(jnp.mean(ot * ot, axis=0, keepdims=True) + SUBLN_EPS)
    o_ref[0] = (ot.T * g_ref[...] * (1.0 - lam_init)).astype(BF16)


def _attn(q3, k3, vt3, lam_params, subln_g, lam_init, *, tq=1024, tk=512):
    b, s, _ = q3.shape
    hd = ATTN_V_DIM
    assert tq % tk == 0 and s % tq == 0
    n_diag = tq // tk
    coef, q_tab, k_tab = _alibi_tables(s)
    key_pos = (jnp.arange(n_diag)[:, None, None] * tk + jnp.arange(tk)[None, :, None])
    dist = jnp.abs(key_pos - jnp.arange(tq)[None, None, :]).astype(F32)
    diag_bias = -coef[:, None, None, None] * dist
    k_spec = pl.BlockSpec((1, s, hd), lambda bi, hi, qi: (bi, 0, hi))
    vt_spec = pl.BlockSpec((1, hd, s), lambda bi, hi, qi: (bi, hi, 0))
    q_spec = pl.BlockSpec((1, tq, hd), lambda bi, hi, qi: (bi, qi, hi))
    return pl.pallas_call(
        functools.partial(_attn_body, tq=tq, tk=tk, seq=s, lam_init=lam_init),
        grid=(b, ATTN_HEADS, s // tq),
        in_specs=[_const_spec(lam_params.shape),
                  _const_spec((1, hd)),
                  q_spec,
                  pl.BlockSpec((1, tq, LANES), lambda bi, hi, qi: (hi, qi, 0)),
                  pl.BlockSpec((1, n_diag, tk, tq), lambda bi, hi, qi: (hi, 0, 0, 0)),
                  k_spec,
                  pl.BlockSpec((1, s, LANES), lambda bi, hi, qi: (hi, 0, 0)),
                  vt_spec],
        out_specs=q_spec,
        out_shape=jax.ShapeDtypeStruct((b, s, ATTN_WIDTH), BF16),
        scratch_shapes=[pltpu.VMEM((hd + SUBLANES, tq), F32), pltpu.VMEM((hd + SUBLANES, tq), F32),
                        pltpu.VMEM((SUBLANES, LANES), F32)],
        compiler_params=_params("parallel", "parallel", "arbitrary"),
        name="diff_attn",
    )(lam_params, subln_g.reshape(1, hd), q3, q_tab, diag_bias, k3, k_tab, vt3)


def _merge_body(x_ref, g_ref, wgate_ref, mb_ref, pool_ref, attn_ref, lru_ref,
                wbp_ref, wba_ref, wbl_ref, wout_ref, o_ref):
    x = x_ref[...]
    d = x.shape[-1]
    h = _rms(x, g_ref[...], NORM_EPS).astype(BF16)
    merged = None
    for i, (br_ref, wb_ref) in enumerate(((pool_ref, wbp_ref), (attn_ref, wba_ref), (lru_ref, wbl_ref))):
        logits = _dot(h, wgate_ref[:, i * d:(i + 1) * d]) + mb_ref[i:i + 1, :]
        term = jax.nn.sigmoid(logits) * _dot(br_ref[...], wb_ref[...])
        merged = term if merged is None else merged + term
    o_ref[...] = x + _dot(merged.astype(BF16), wout_ref[...])


def _merge(x2, g, w_gate, merge_bias, pool2, attn2, lru2, wbp, wba, wbl, w_out, *, tm=1024):
    n, d = x2.shape

    def tok(width):
        return pl.BlockSpec((tm, width), lambda i: (i, 0))

    return pl.pallas_call(
        _merge_body,
        grid=(n // tm,),
        in_specs=[tok(d), _const_spec((1, d)), _const_spec(w_gate.shape), _const_spec(merge_bias.shape),
                  tok(pool2.shape[1]), tok(attn2.shape[1]), tok(lru2.shape[1]),
                  _const_spec(wbp.shape), _const_spec(wba.shape), _const_spec(wbl.shape),
                  _const_spec(w_out.shape)],
        out_specs=tok(d),
        out_shape=jax.ShapeDtypeStruct((n, d), F32),
        compiler_params=_params("parallel"),
        name="merge",
    )(x2, g.reshape(1, d), w_gate, merge_bias, pool2, attn2, lru2, wbp, wba, wbl, w_out)


def _block_diag(w):
    g, c, _ = w.shape
    eye = jnp.eye(g, dtype=w.dtype)
    return (eye[:, None, :, None] * w[:, :, None, :]).reshape(g * c, g * c)


def kernel(x, ffn1_norm, ffn1_w_in, ffn1_w_out, mix_norm, w_in, pool_w, pool_scale, attn_lambda, attn_subln, lru_conv_w, lru_conv_b, lru_w_a, lru_b_a, lru_w_x, lru_b_x, lru_lambda, w_branch_pool, w_branch_attn, w_branch_lru, merge_bias, w_out, ffn2_norm, ffn2_w_in, ffn2_w_out, final_norm):
    b, s, d = x.shape
    depth = ffn1_norm.shape[0]
    qk = ATTN_HEADS * 2 * ATTN_QK_DIM
    mix_cols = POOL_WIDTH + 2 * qk + ATTN_WIDTH + 2 * LRU_WIDTH
    x2 = x.reshape(b * s, d)
    for l in range(depth):
        lam_init = 0.8 - 0.6 * math.exp(-0.3 * l)
        last = l == depth - 1
        x2 = _ffn(x2, ffn1_norm[l], ffn1_w_in[l].astype(BF16), ffn1_w_out[l].astype(BF16))

        p, q, k, vt, lx, lg = _inproj(x2, mix_norm[l], w_in[l, :, :mix_cols].astype(BF16), s)
        pool = _pool(p.reshape(b, s, -1), _block_diag(pool_w[l]).astype(BF16), pool_scale[l])
        w_gates = jnp.concatenate(
            [_block_diag(lru_w_a[l, 0]), _block_diag(lru_w_x[l, 0]),
             _block_diag(lru_w_a[l, 1]), _block_diag(lru_w_x[l, 1])], axis=1).astype(BF16)
        b_gates = jnp.concatenate([lru_b_a[l, 0], lru_b_x[l, 0], lru_b_a[l, 1], lru_b_x[l, 1]])
        lru = _lru(lx.reshape(b, s, -1), lg.reshape(b, s, -1), lru_conv_w[l], lru_conv_b[l],
                   w_gates, b_gates, lru_lambda[l])
        attn = _attn(q.reshape(b, s, -1), k.reshape(b, s, -1), vt,
                     attn_lambda[l], attn_subln[l], lam_init)
        x2 = _merge(x2, mix_norm[l], w_in[l, :, mix_cols:].astype(BF16), merge_bias[l],
                    pool.reshape(b * s, -1), attn.reshape(b * s, -1), lru.reshape(b * s, -1),
                    w_branch_pool[l].astype(BF16), w_branch_attn[l].astype(BF16),
                    w_branch_lru[l].astype(BF16), w_out[l].astype(BF16))

        x2 = _ffn(x2, ffn2_norm[l], ffn2_w_in[l].astype(BF16), ffn2_w_out[l].astype(BF16),
                  final_norm if last else None)
    return x2.reshape(b, s, d)
```

```python
import functools
import math

import jax
import jax.numpy as jnp
from jax import lax
from jax.experimental import pallas as pl
from jax.experimental.pallas import tpu as pltpu

F32 = jnp.float32
BF16 = jnp.bfloat16

NORM_EPS = 1e-6
SUBLN_EPS = 1e-5
LRU_C = 8.0

POOL_WINDOWS = (2, 4, 8, 16)
POOL_WIDTH = 256
ATTN_HEADS = 4
ATTN_QK_DIM = 64
ATTN_V_DIM = 128
ATTN_WIDTH = ATTN_HEADS * ATTN_V_DIM
LRU_WIDTH = 256
LRU_CONV_WIDTH = 4

LOG2E = math.log2(math.e)
Q_SCALE = ATTN_QK_DIM ** -0.5 * LOG2E

LANES = 128
SUBLANES = 8
MXU_COLS = 256
VMEM_LIMIT_BYTES = 52 * 1024 * 1024


def _params(*sem):
    return pltpu.CompilerParams(dimension_semantics=sem,
                                vmem_limit_bytes=VMEM_LIMIT_BYTES)


def _const_spec(shape):
    nd = len(shape)
    return pl.BlockSpec(shape, lambda *_: (0,) * nd, pipeline_mode=pl.Buffered(1))


def _rms(x, g, eps):
    ms = jnp.mean(x * x, axis=-1, keepdims=True)
    return x * lax.rsqrt(ms + eps) * g


def _dot(a, b):
    return jnp.dot(a, b, preferred_element_type=F32)


def _dot_nt(a, b):
    return lax.dot_general(a, b, (((1,), (1,)), ((), ())), preferred_element_type=F32)


def _ffn_body(*refs, d_ff, chunk, final):
    if final:
        x_ref, g_ref, win_ref, wout_ref, fg_ref, o_ref, acc_ref = refs
    else:
        x_ref, g_ref, win_ref, wout_ref, o_ref, acc_ref = refs
    x = x_ref[...]
    h = _rms(x, g_ref[...], NORM_EPS).astype(BF16)
    for c in range(d_ff // chunk):
        lo = c * chunk
        gate = _dot(h, win_ref[:, lo:lo + chunk])
        up = _dot(h, win_ref[:, d_ff + lo:d_ff + lo + chunk])
        act = (gate * jax.nn.sigmoid(gate) * up).astype(BF16)
        part = _dot(act, wout_ref[lo:lo + chunk, :])
        if c == 0:
            acc_ref[...] = part
        else:
            acc_ref[...] += part
    y = x + 0.5 * acc_ref[...]
    if final:
        y = _rms(y, fg_ref[...], NORM_EPS)
    o_ref[...] = y


def _ffn(x2, g, w_in, w_out, final_g=None, *, tm=1024):
    n, d = x2.shape
    d_ff = w_out.shape[0]
    final = final_g is not None
    in_specs = [
        pl.BlockSpec((tm, d), lambda i: (i, 0)),
        _const_spec((1, d)),
        _const_spec(w_in.shape),
        _const_spec(w_out.shape),
    ]
    args = [x2, g.reshape(1, d), w_in, w_out]
    if final:
        in_specs.append(_const_spec((1, d)))
        args.append(final_g.reshape(1, d))
    return pl.pallas_call(
        functools.partial(_ffn_body, d_ff=d_ff, chunk=MXU_COLS, final=final),
        grid=(n // tm,),
        in_specs=in_specs,
        out_specs=pl.BlockSpec((tm, d), lambda i: (i, 0)),
        out_shape=jax.ShapeDtypeStruct((n, d), F32),
        scratch_shapes=[pltpu.VMEM((tm, d), F32)],
        compiler_params=_params("parallel"),
        name="ffn_final" if final else "ffn",
    )(*args)


def _inproj_body(x_ref, g_ref, w_ref, p_ref, q_ref, k_ref, vt_ref, lx_ref, lg_ref):
    h = _rms(x_ref[...], g_ref[...], NORM_EPS).astype(BF16)
    qk = ATTN_HEADS * 2 * ATTN_QK_DIM
    off = 0

    def proj(width):
        nonlocal off
        out = _dot(h, w_ref[:, off:off + width])
        off += width
        return out

    p_ref[...] = proj(POOL_WIDTH)
    q_ref[...] = (proj(qk) * Q_SCALE).astype(BF16)
    k_ref[...] = proj(qk).astype(BF16)
    vt_ref[0] = proj(ATTN_WIDTH).T.astype(BF16)
    lx_ref[...] = proj(LRU_WIDTH)
    lg_ref[...] = proj(LRU_WIDTH)


def _inproj(x2, g, w, seq, *, tm=1024):
    n, d = x2.shape
    assert seq % tm == 0
    tiles = seq // tm
    qk = ATTN_HEADS * 2 * ATTN_QK_DIM

    def tok(width, dtype):
        return pl.BlockSpec((tm, width), lambda i: (i, 0)), jax.ShapeDtypeStruct((n, width), dtype)

    vt = (pl.BlockSpec((1, ATTN_WIDTH, tm), lambda i: (i // tiles, 0, i % tiles)),
          jax.ShapeDtypeStruct((n // seq, ATTN_WIDTH, seq), BF16))
    outs = [tok(POOL_WIDTH, F32), tok(qk, BF16), tok(qk, BF16), vt, tok(LRU_WIDTH, F32), tok(LRU_WIDTH, F32)]
    return pl.pallas_call(
        _inproj_body,
        grid=(n // tm,),
        in_specs=[pl.BlockSpec((tm, d), lambda i: (i, 0)),
                  _const_spec((1, d)),
                  _const_spec(w.shape)],
        out_specs=[spec for spec, _ in outs],
        out_shape=[shape for _, shape in outs],
        compiler_params=_params("parallel"),
        name="inproj",
    )(x2, g.reshape(1, d), w)


POOL_PAD = 16


def _pool_body(p_ref, w_ref, sc_ref, o_ref, pad_ref, *, seq, rows):
    zeros = jnp.zeros((POOL_PAD, POOL_WIDTH), F32)
    pad_ref[0:POOL_PAD, :] = zeros
    pad_ref[seq + POOL_PAD:seq + 2 * POOL_PAD, :] = zeros
    pad_ref[POOL_PAD:seq + POOL_PAD, :] = p_ref[0]

    low = lax.broadcasted_iota(jnp.int32, (rows, LANES), 1) < LANES // 2
    row = lax.broadcasted_iota(jnp.int32, (rows, LANES), 0)

    def doubled(a, d):
        return a[d:, :] + a[:a.shape[0] - d, :]

    def chunk(c, carry):
        t0 = pl.multiple_of(c * rows, rows)
        xs = pad_ref[pl.ds(t0, rows + 2 * POOL_PAD), :]
        xa, xb = xs[:, :LANES], xs[:, LANES:]
        a2 = doubled(xa, 1)
        a4 = doubled(a2, 2)
        b8 = doubled(doubled(doubled(xb, 1), 2), 4)
        b16 = doubled(b8, 8)
        sum_a = jnp.where(low, a2[POOL_PAD - 1:POOL_PAD - 1 + rows], a4[POOL_PAD - 2:POOL_PAD - 2 + rows])
        sum_b = jnp.where(low, b8[POOL_PAD - 4:POOL_PAD - 4 + rows], b16[POOL_PAD - 8:POOL_PAD - 8 + rows])
        t = t0 + row

        def count(half):
            return (jnp.minimum(t + half, seq) - jnp.maximum(t - half, 0)).astype(F32)

        cnt_a = jnp.where(low, count(1), count(2))
        cnt_b = jnp.where(low, count(4), count(8))
        mix_a = sum_a / cnt_a - xa[POOL_PAD:POOL_PAD + rows]
        mix_b = sum_b / cnt_b - xb[POOL_PAD:POOL_PAD + rows]
        mixed = jnp.concatenate([mix_a, mix_b], axis=1).astype(BF16)
        y = _dot(mixed, w_ref[...]) * sc_ref[...]
        o_ref[0, pl.ds(t0, rows), :] = y.astype(BF16)
        return carry

    lax.fori_loop(0, seq // rows, chunk, 0)


def _pool(p3, w_bd, scale, *, rows=512):
    b, s, c = p3.shape
    return pl.pallas_call(
        functools.partial(_pool_body, seq=s, rows=rows),
        grid=(b,),
        in_specs=[pl.BlockSpec((1, s, c), lambda i: (i, 0, 0)),
                  _const_spec(w_bd.shape),
                  _const_spec((1, c))],
        out_specs=pl.BlockSpec((1, s, c), lambda i: (i, 0, 0)),
        out_shape=jax.ShapeDtypeStruct((b, s, c), BF16),
        scratch_shapes=[pltpu.VMEM((s + 2 * POOL_PAD, c), F32)],
        compiler_params=_params("parallel"),
        name="pool",
    )(p3, w_bd, scale.reshape(1, c))


LRU_PAD = 8
SCAN_ROWS = 8


def _softplus(x):
    return jnp.maximum(x, 0.0) + jnp.log1p(jnp.exp(-jnp.abs(x)))


def _lru_body(lx_ref, lg_ref, cw_ref, cb_ref, wg_ref, bg_ref, lam_ref, o_ref,
              pad_ref, af_ref, uf_ref, ab_ref, ub_ref, *, seq, rows):
    w = LRU_WIDTH
    zeros = jnp.zeros((LRU_PAD, w), F32)
    pad_ref[0:LRU_PAD, :] = zeros
    pad_ref[seq + LRU_PAD:seq + 2 * LRU_PAD, :] = zeros
    pad_ref[LRU_PAD:seq + LRU_PAD, :] = lx_ref[0]

    decay = -LRU_C * _softplus(-lam_ref[...])
    lead = LRU_CONV_WIDTH // 2
    row_in_blk = lax.broadcasted_iota(jnp.int32, (rows, w), 0) % SCAN_ROWS

    def blk_shift(x, d, fill, reverse):
        blocks = x.reshape(rows // SCAN_ROWS, SCAN_ROWS, w)
        if reverse:
            moved = pltpu.roll(blocks, SCAN_ROWS - d, axis=1).reshape(rows, w)
            return jnp.where(row_in_blk < SCAN_ROWS - d, moved, fill)
        moved = pltpu.roll(blocks, d, axis=1).reshape(rows, w)
        return jnp.where(row_in_blk >= d, moved, fill)

    def gates(c, carry):
        t0 = pl.multiple_of(c * rows, rows)
        xs = pad_ref[pl.ds(t0, rows + 2 * LRU_PAD), :]
        xf = cb_ref[...]
        for j in range(LRU_CONV_WIDTH):
            s0 = LRU_PAD - lead + j
            xf = xf + cw_ref[j:j + 1, :] * xs[s0:s0 + rows]
        g = _dot(xf.astype(BF16), wg_ref[...]) + bg_ref[...]
        for d, (a_ref, u_ref) in enumerate(((af_ref, uf_ref), (ab_ref, ub_ref))):
            r = jax.nn.sigmoid(g[:, (2 * d) * w:(2 * d + 1) * w])
            i = jax.nn.sigmoid(g[:, (2 * d + 1) * w:(2 * d + 2) * w])
            log_a = decay[d:d + 1, :] * r
            a = jnp.exp(log_a)
            one_minus_a2 = -jnp.tanh(log_a) * (a * a + 1.0)
            u = jnp.sqrt(one_minus_a2) * (i * xf)
            step = 1
            while step < SCAN_ROWS:
                u = u + a * blk_shift(u, step, 0.0, d == 1)
                a = a * blk_shift(a, step, 1.0, d == 1)
                step *= 2
            a_ref[pl.ds(t0, rows), :] = a
            u_ref[pl.ds(t0, rows), :] = u
        return carry

    lax.fori_loop(0, seq // rows, gates, 0)

    def carry_blocks(i, carry):
        hf, hb = carry
        base_f = pl.multiple_of(i * SCAN_ROWS, SCAN_ROWS)
        base_b = pl.multiple_of(seq - SCAN_ROWS - i * SCAN_ROWS, SCAN_ROWS)
        tf = uf_ref[pl.ds(base_f, SCAN_ROWS), :] + af_ref[pl.ds(base_f, SCAN_ROWS), :] * hf
        uf_ref[pl.ds(base_f, SCAN_ROWS), :] = tf
        tb = ub_ref[pl.ds(base_b, SCAN_ROWS), :] + ab_ref[pl.ds(base_b, SCAN_ROWS), :] * hb
        ub_ref[pl.ds(base_b, SCAN_ROWS), :] = tb
        return tf[SCAN_ROWS - 1:SCAN_ROWS, :], tb[0:1, :]

    h0 = jnp.zeros((1, w), F32)
    lax.fori_loop(0, seq // SCAN_ROWS, carry_blocks, (h0, h0), unroll=8)

    def gate_out(c, carry):
        t0 = pl.multiple_of(c * rows, rows)
        h = uf_ref[pl.ds(t0, rows), :] + ub_ref[pl.ds(t0, rows), :]
        y = jax.nn.gelu(lg_ref[0, pl.ds(t0, rows), :]) * h
        o_ref[0, pl.ds(t0, rows), :] = y.astype(BF16)
        return carry

    lax.fori_loop(0, seq // rows, gate_out, 0)


def _lru(lx3, lg3, conv_w, conv_b, w_gates, b_gates, lam, *, rows=512):
    b, s, c = lx3.shape
    seq_spec = pl.BlockSpec((1, s, c), lambda i: (i, 0, 0))
    return pl.pallas_call(
        functools.partial(_lru_body, seq=s, rows=rows),
        grid=(b,),
        in_specs=[seq_spec, seq_spec,
                  _const_spec(conv_w.shape), _const_spec((1, c)),
                  _const_spec(w_gates.shape), _const_spec((1, 4 * c)),
                  _const_spec(lam.shape)],
        out_specs=seq_spec,
        out_shape=jax.ShapeDtypeStruct((b, s, c), BF16),
        scratch_shapes=[pltpu.VMEM((s + 2 * LRU_PAD, c), F32)]
        + [pltpu.VMEM((s, c), F32) for _ in range(4)],
        compiler_params=_params("parallel"),
        name="lru",
    )(lx3, lg3, conv_w, conv_b.reshape(1, c), w_gates, b_gates.reshape(1, 4 * c), lam)


ALIBI_SPLIT = 64
SKEW = 2
SCORE_LIMIT = 32.0
NORM_MARGIN = 0.98
VALUE_LIMIT = 2.0 ** 80

def _bf16_split3(c):
    pieces = []
    for _ in range(3):
        p = c.astype(BF16).astype(F32)
        pieces.append(p)
        c = c - p
    return pieces


def _alibi_tables(seq):
    assert seq <= ALIBI_SPLIT * 256
    slopes = jnp.asarray([2.0 ** (-8.0 / ATTN_HEADS * (i + 1)) for i in range(ATTN_HEADS)], F32)
    coef = slopes * LOG2E
    pos = jnp.arange(seq)
    parts = [jnp.broadcast_to((ALIBI_SPLIT * (pos // ALIBI_SPLIT)).astype(F32), (ATTN_HEADS, seq)),
             jnp.broadcast_to((pos % ALIBI_SPLIT).astype(F32), (ATTN_HEADS, seq))]
    cs = [jnp.broadcast_to(c[:, None], (ATTN_HEADS, seq)) for c in _bf16_split3(coef)]
    q_cols = [part for part in parts for _ in cs] + [c for _ in parts for c in cs]
    k_cols = [-c for _ in parts for c in cs] + [part for part in parts for _ in cs]
    pad = ((0, 0), (0, 0), (0, LANES - len(q_cols)))
    q_tab = jnp.pad(jnp.stack(q_cols, axis=-1), pad).astype(BF16)
    k_tab = jnp.pad(jnp.stack(k_cols, axis=-1), pad).astype(BF16)
    return coef, q_tab, k_tab


def _attn_body(lam_ref, g_ref, q_ref, qtab_ref, dbias_ref, k_ref, ktab_ref, vt_ref, o_ref,
               acc1_ref, acc2_ref, stat_ref, *, tq, tk, seq, lam_init):
    n_diag = tq // tk
    diag = pl.program_id(2) * n_diag

    qf = q_ref[0].astype(F32)
    lane = lax.broadcasted_iota(jnp.int32, qf.shape, 1)
    q1 = jnp.where(lane < ATTN_QK_DIM, qf, 0.0).astype(BF16)
    q2 = jnp.where(lane >= ATTN_QK_DIM, qf, 0.0).astype(BF16)
    tab_before = qtab_ref[0]
    tab_after = (-tab_before.astype(F32)).astype(BF16)

    same_map = ((lax.broadcasted_iota(jnp.int32, (LANES, LANES), 0) < ATTN_QK_DIM)
                == (lax.broadcasted_iota(jnp.int32, (LANES, LANES), 1) < ATTN_QK_DIM))
    map_ones = jnp.where(same_map, 1.0, 0.0).astype(BF16)

    def max_sq_norm(x):
        return jnp.max(_dot((x * x).astype(BF16), map_ones), axis=0, keepdims=True)

    @pl.when(pl.program_id(2) == 0)
    def _():
        stat_ref[0:1, :] = max_sq_norm(k_ref[0].astype(F32))
        stat_ref[1:2, :] = jnp.full((1, LANES), jnp.max(jnp.abs(vt_ref[0].astype(F32))), F32)

    excess = jnp.maximum(max_sq_norm(qf) * stat_ref[0:1, :] - SCORE_LIMIT ** 2 * NORM_MARGIN,
                         stat_ref[1:2, :] - VALUE_LIMIT)
    plain_softmax_ok = jnp.max(excess) <= 0.0

    acc1_ref[...] = jnp.zeros_like(acc1_ref)
    acc2_ref[...] = jnp.zeros_like(acc2_ref)

    hd = ATTN_V_DIM

    def weighted(p, k0):
        pv = _dot(vt_ref[0, :, pl.ds(k0, tk)], p.astype(BF16))
        return jnp.concatenate([pv, jnp.sum(p.reshape(tk // SUBLANES, SUBLANES, tq), axis=0)], axis=0)

    def update(s, m, acc_ref, k0):
        m_new = jnp.maximum(m, jnp.max(s, axis=0, keepdims=True))
        acc_ref[...] = jnp.exp2(m - m_new) * acc_ref[...] + weighted(jnp.exp2(s - m_new), k0)
        return m_new

    def update_plain(s, m, acc_ref, k0):
        acc_ref[...] += weighted(jnp.exp2(s), k0)
        return m

    def folded_scores(jj):
        j = jj + n_diag * (jj >= diag).astype(jnp.int32)
        tab = jnp.where(j > diag, tab_after, tab_before)
        k0 = pl.multiple_of(j * tk, tk)
        ka = jnp.concatenate([k_ref[0, pl.ds(k0, tk), :], ktab_ref[0, pl.ds(k0, tk), :]], axis=1)
        return (_dot_nt(ka, jnp.concatenate([q1, tab], axis=1)),
                _dot_nt(ka, jnp.concatenate([q2, tab], axis=1)), k0)

    def diag_scores(r):
        k0 = pl.multiple_of((diag + r) * tk, tk)
        kc = k_ref[0, pl.ds(k0, tk), :]
        return _dot_nt(kc, q1) + dbias_ref[0, r], _dot_nt(kc, q2) + dbias_ref[0, r], k0

    def all_chunks(step_fn):
        m1 = m2 = jnp.full((1, tq), -jnp.inf, F32)
        pending = []
        for step in range(seq // tk):
            pending.append(diag_scores(step) if step < n_diag else folded_scores(jnp.int32(step - n_diag)))
            if len(pending) > SKEW:
                s1, s2, k0 = pending.pop(0)
                m1 = step_fn(s1, m1, acc1_ref, k0)
                m2 = step_fn(s2, m2, acc2_ref, k0)
        for s1, s2, k0 in pending:
            m1 = step_fn(s1, m1, acc1_ref, k0)
            m2 = step_fn(s2, m2, acc2_ref, k0)

    @pl.when(plain_softmax_ok)
    def _():
        all_chunks(update_plain)

    @pl.when(jnp.logical_not(plain_softmax_ok))
    def _():
        all_chunks(update)

    lp = lam_ref[...]
    lam = (jnp.exp(jnp.sum(lp[0:1] * lp[1:2], axis=-1, keepdims=True))
           - jnp.exp(jnp.sum(lp[2:3] * lp[3:4], axis=-1, keepdims=True)) + lam_init)
    inv1 = 1.0 / jnp.sum(acc1_ref[hd:, :], axis=0, keepdims=True)
    inv2 = lam / jnp.sum(acc2_ref[hd:, :], axis=0, keepdims=True)
    ot = acc1_ref[:hd, :] * inv1 - acc2_ref[:hd, :] * inv2
    ot = ot * lax.rsqrt(jnp.mean(ot * ot, axis=0, keepdims=True) + SUBLN_EPS)
    o_ref[0] = (ot.T * (g_ref[...] * (1.0 - lam_init))).astype(BF16)


def _attn(q3, k3, vt3, lam_params, subln_g, lam_init, *, tq=1024, tk=1024):
    b, s, _ = q3.shape
    hd = ATTN_V_DIM
    assert tq % tk == 0 and s % tq == 0
    n_diag = tq // tk
    coef, q_tab, k_tab = _alibi_tables(s)
    key_pos = (jnp.arange(n_diag)[:, None, None] * tk + jnp.arange(tk)[None, :, None])
    dist = jnp.abs(key_pos - jnp.arange(tq)[None, None, :]).astype(F32)
    diag_bias = -coef[:, None, None, None] * dist
    k_spec = pl.BlockSpec((1, s, hd), lambda bi, hi, qi: (bi, 0, hi))
    vt_spec = pl.BlockSpec((1, hd, s), lambda bi, hi, qi: (bi, hi, 0))
    q_spec = pl.BlockSpec((1, tq, hd), lambda bi, hi, qi: (bi, qi, hi))
    return pl.pallas_call(
        functools.partial(_attn_body, tq=tq, tk=tk, seq=s, lam_init=lam_init),
        grid=(b, ATTN_HEADS, s // tq),
        in_specs=[_const_spec(lam_params.shape),
                  _const_spec((1, hd)),
                  q_spec,
                  pl.BlockSpec((1, tq, LANES), lambda bi, hi, qi: (hi, qi, 0)),
                  pl.BlockSpec((1, n_diag, tk, tq), lambda bi, hi, qi: (hi, 0, 0, 0)),
                  k_spec,
                  pl.BlockSpec((1, s, LANES), lambda bi, hi, qi: (hi, 0, 0)),
                  vt_spec],
        out_specs=q_spec,
        out_shape=jax.ShapeDtypeStruct((b, s, ATTN_WIDTH), BF16),
        scratch_shapes=[pltpu.VMEM((hd + SUBLANES, tq), F32), pltpu.VMEM((hd + SUBLANES, tq), F32),
                        pltpu.VMEM((SUBLANES, LANES), F32)],
        compiler_params=_params("parallel", "parallel", "arbitrary"),
        name="diff_attn",
    )(lam_params, subln_g.reshape(1, hd), q3, q_tab, diag_bias, k3, k_tab, vt3)


def _merge_body(x_ref, g_ref, wgate_ref, mb_ref, pool_ref, attn_ref, lru_ref,
                wbp_ref, wba_ref, wbl_ref, wout_ref, o_ref):
    x = x_ref[...]
    d = x.shape[-1]
    h = _rms(x, g_ref[...], NORM_EPS).astype(BF16)
    merged = None
    for i, (br_ref, wb_ref) in enumerate(((pool_ref, wbp_ref), (attn_ref, wba_ref), (lru_ref, wbl_ref))):
        logits = _dot(h, wgate_ref[:, i * d:(i + 1) * d]) + mb_ref[i:i + 1, :]
        term = jax.nn.sigmoid(logits) * _dot(br_ref[...], wb_ref[...])
        merged = term if merged is None else merged + term
    o_ref[...] = x + _dot(merged.astype(BF16), wout_ref[...])


def _merge(x2, g, w_gate, merge_bias, pool2, attn2, lru2, wbp, wba, wbl, w_out, *, tm=1024):
    n, d = x2.shape

    def tok(width):
        return pl.BlockSpec((tm, width), lambda i: (i, 0))

    return pl.pallas_call(
        _merge_body,
        grid=(n // tm,),
        in_specs=[tok(d), _const_spec((1, d)), _const_spec(w_gate.shape), _const_spec(merge_bias.shape),
                  tok(pool2.shape[1]), tok(attn2.shape[1]), tok(lru2.shape[1]),
                  _const_spec(wbp.shape), _const_spec(wba.shape), _const_spec(wbl.shape),
                  _const_spec(w_out.shape)],
        out_specs=tok(d),
        out_shape=jax.ShapeDtypeStruct((n, d), F32),
        compiler_params=_params("parallel"),
        name="merge",
    )(x2, g.reshape(1, d), w_gate, merge_bias, pool2, attn2, lru2, wbp, wba, wbl, w_out)


def _block_diag(w):
    g, c, _ = w.shape
    eye = jnp.eye(g, dtype=w.dtype)
    return (eye[:, None, :, None] * w[:, :, None, :]).reshape(g * c, g * c)


def kernel(x, ffn1_norm, ffn1_w_in, ffn1_w_out, mix_norm, w_in, pool_w, pool_scale, attn_lambda, attn_subln, lru_conv_w, lru_conv_b, lru_w_a, lru_b_a, lru_w_x, lru_b_x, lru_lambda, w_branch_pool, w_branch_attn, w_branch_lru, merge_bias, w_out, ffn2_norm, ffn2_w_in, ffn2_w_out, final_norm):
    b, s, d = x.shape
    depth = ffn1_norm.shape[0]
    qk = ATTN_HEADS * 2 * ATTN_QK_DIM
    mix_cols = POOL_WIDTH + 2 * qk + ATTN_WIDTH + 2 * LRU_WIDTH
    x2 = x.reshape(b * s, d)
    for l in range(depth):
        lam_init = 0.8 - 0.6 * math.exp(-0.3 * l)
        last = l == depth - 1
        x2 = _ffn(x2, ffn1_norm[l], ffn1_w_in[l].astype(BF16), ffn1_w_out[l].astype(BF16))

        p, q, k, vt, lx, lg = _inproj(x2, mix_norm[l], w_in[l, :, :mix_cols].astype(BF16), s)
        pool = _pool(p.reshape(b, s, -1), _block_diag(pool_w[l]).astype(BF16), pool_scale[l])
        w_gates = jnp.concatenate(
            [_block_diag(lru_w_a[l, 0]), _block_diag(lru_w_x[l, 0]),
             _block_diag(lru_w_a[l, 1]), _block_diag(lru_w_x[l, 1])], axis=1).astype(BF16)
        b_gates = jnp.concatenate([lru_b_a[l, 0], lru_b_x[l, 0], lru_b_a[l, 1], lru_b_x[l, 1]])
        lru = _lru(lx.reshape(b, s, -1), lg.reshape(b, s, -1), lru_conv_w[l], lru_conv_b[l],
                   w_gates, b_gates, lru_lambda[l])
        attn = _attn(q.reshape(b, s, -1), k.reshape(b, s, -1), vt,
                     attn_lambda[l], attn_subln[l], lam_init)
        x2 = _merge(x2, mix_norm[l], w_in[l, :, mix_cols:].astype(BF16), merge_bias[l],
                    pool.reshape(b * s, -1), attn.reshape(b * s, -1), lru.reshape(b * s, -1),
                    w_branch_pool[l].astype(BF16), w_branch_attn[l].astype(BF16),
                    w_branch_lru[l].astype(BF16), w_out[l].astype(BF16))

        x2 = _ffn(x2, ffn2_norm[l], ffn2_w_in[l].astype(BF16), ffn2_w_out[l].astype(BF16),
                  final_norm if last else None)
    return x2.reshape(b, s, d)
```

```python
import functools
import math

import jax
import jax.numpy as jnp
from jax import lax
from jax.experimental import pallas as pl
from jax.experimental.pallas import tpu as pltpu

F32 = jnp.float32
BF16 = jnp.bfloat16

NORM_EPS = 1e-6
SUBLN_EPS = 1e-5
LRU_C = 8.0

POOL_WINDOWS = (2, 4, 8, 16)
POOL_WIDTH = 256
ATTN_HEADS = 4
ATTN_QK_DIM = 64
ATTN_V_DIM = 128
ATTN_WIDTH = ATTN_HEADS * ATTN_V_DIM
LRU_WIDTH = 256
LRU_CONV_WIDTH = 4

LOG2E = math.log2(math.e)
Q_SCALE = ATTN_QK_DIM ** -0.5 * LOG2E

LANES = 128
SUBLANES = 8
MXU_COLS = 256
VMEM_LIMIT_BYTES = 52 * 1024 * 1024


def _params(*sem):
    return pltpu.CompilerParams(dimension_semantics=sem,
                                vmem_limit_bytes=VMEM_LIMIT_BYTES)


def _const_spec(shape):
    nd = len(shape)
    return pl.BlockSpec(shape, lambda *_: (0,) * nd, pipeline_mode=pl.Buffered(1))


def _rms(x, g, eps):
    ms = jnp.mean(x * x, axis=-1, keepdims=True)
    return x * lax.rsqrt(ms + eps) * g


def _dot(a, b):
    return jnp.dot(a, b, preferred_element_type=F32)


def _dot_nt(a, b):
    return lax.dot_general(a, b, (((1,), (1,)), ((), ())), preferred_element_type=F32)


def _ffn_body(*refs, d_ff, chunk, final):
    if final:
        x_ref, g_ref, win_ref, wout_ref, fg_ref, o_ref, acc_ref = refs
    else:
        x_ref, g_ref, win_ref, wout_ref, o_ref, acc_ref = refs
    x = x_ref[...]
    h = _rms(x, g_ref[...], NORM_EPS).astype(BF16)
    for c in range(d_ff // chunk):
        lo = c * chunk
        gate = _dot(h, win_ref[:, lo:lo + chunk])
        up = _dot(h, win_ref[:, d_ff + lo:d_ff + lo + chunk])
        act = (gate * jax.nn.sigmoid(gate) * up).astype(BF16)
        part = _dot(act, wout_ref[lo:lo + chunk, :])
        if c == 0:
            acc_ref[...] = part
        else:
            acc_ref[...] += part
    y = x + 0.5 * acc_ref[...]
    if final:
        y = _rms(y, fg_ref[...], NORM_EPS)
    o_ref[...] = y


def _ffn(x2, g, w_in, w_out, final_g=None, *, tm=1024):
    n, d = x2.shape
    d_ff = w_out.shape[0]
    final = final_g is not None
    in_specs = [
        pl.BlockSpec((tm, d), lambda i: (i, 0)),
        _const_spec((1, d)),
        _const_spec(w_in.shape),
        _const_spec(w_out.shape),
    ]
    args = [x2, g.reshape(1, d), w_in, w_out]
    if final:
        in_specs.append(_const_spec((1, d)))
        args.append(final_g.reshape(1, d))
    return pl.pallas_call(
        functools.partial(_ffn_body, d_ff=d_ff, chunk=MXU_COLS, final=final),
        grid=(n // tm,),
        in_specs=in_specs,
        out_specs=pl.BlockSpec((tm, d), lambda i: (i, 0)),
        out_shape=jax.ShapeDtypeStruct((n, d), F32),
        scratch_shapes=[pltpu.VMEM((tm, d), F32)],
        compiler_params=_params("parallel"),
        name="ffn_final" if final else "ffn",
    )(*args)


def _inproj_body(x_ref, g_ref, w_ref, p_ref, q_ref, k_ref, vt_ref, lx_ref, lg_ref):
    h = _rms(x_ref[...], g_ref[...], NORM_EPS).astype(BF16)
    qk = ATTN_HEADS * 2 * ATTN_QK_DIM
    off = 0

    def proj(width):
        nonlocal off
        out = _dot(h, w_ref[:, off:off + width])
        off += width
        return out

    p_ref[...] = proj(POOL_WIDTH)
    q_ref[...] = (proj(qk) * Q_SCALE).astype(BF16)
    k_ref[...] = proj(qk).astype(BF16)
    vt_ref[0] = proj(ATTN_WIDTH).T.astype(BF16)
    lx_ref[...] = proj(LRU_WIDTH)
    lg_ref[...] = proj(LRU_WIDTH)


def _inproj(x2, g, w, seq, *, tm=1024):
    n, d = x2.shape
    assert seq % tm == 0
    tiles = seq // tm
    qk = ATTN_HEADS * 2 * ATTN_QK_DIM

    def tok(width, dtype):
        return pl.BlockSpec((tm, width), lambda i: (i, 0)), jax.ShapeDtypeStruct((n, width), dtype)

    vt = (pl.BlockSpec((1, ATTN_WIDTH, tm), lambda i: (i // tiles, 0, i % tiles)),
          jax.ShapeDtypeStruct((n // seq, ATTN_WIDTH, seq), BF16))
    outs = [tok(POOL_WIDTH, F32), tok(qk, BF16), tok(qk, BF16), vt, tok(LRU_WIDTH, F32), tok(LRU_WIDTH, F32)]
    return pl.pallas_call(
        _inproj_body,
        grid=(n // tm,),
        in_specs=[pl.BlockSpec((tm, d), lambda i: (i, 0)),
                  _const_spec((1, d)),
                  _const_spec(w.shape)],
        out_specs=[spec for spec, _ in outs],
        out_shape=[shape for _, shape in outs],
        compiler_params=_params("parallel"),
        name="inproj",
    )(x2, g.reshape(1, d), w)


POOL_PAD = 16


def _pool_body(p_ref, w_ref, sc_ref, o_ref, pad_ref, *, seq, rows):
    zeros = jnp.zeros((POOL_PAD, POOL_WIDTH), F32)
    pad_ref[0:POOL_PAD, :] = zeros
    pad_ref[seq + POOL_PAD:seq + 2 * POOL_PAD, :] = zeros
    pad_ref[POOL_PAD:seq + POOL_PAD, :] = p_ref[0]

    low = lax.broadcasted_iota(jnp.int32, (rows, LANES), 1) < LANES // 2
    row = lax.broadcasted_iota(jnp.int32, (rows, LANES), 0)

    def doubled(a, d):
        return a[d:, :] + a[:a.shape[0] - d, :]

    def chunk(c, carry):
        t0 = pl.multiple_of(c * rows, rows)
        xs = pad_ref[pl.ds(t0, rows + 2 * POOL_PAD), :]
        xa, xb = xs[:, :LANES], xs[:, LANES:]
        a2 = doubled(xa, 1)
        a4 = doubled(a2, 2)
        b8 = doubled(doubled(doubled(xb, 1), 2), 4)
        b16 = doubled(b8, 8)
        sum_a = jnp.where(low, a2[POOL_PAD - 1:POOL_PAD - 1 + rows], a4[POOL_PAD - 2:POOL_PAD - 2 + rows])
        sum_b = jnp.where(low, b8[POOL_PAD - 4:POOL_PAD - 4 + rows], b16[POOL_PAD - 8:POOL_PAD - 8 + rows])
        t = t0 + row

        def count(half):
            return (jnp.minimum(t + half, seq) - jnp.maximum(t - half, 0)).astype(F32)

        cnt_a = jnp.where(low, count(1), count(2))
        cnt_b = jnp.where(low, count(4), count(8))
        mix_a = sum_a / cnt_a - xa[POOL_PAD:POOL_PAD + rows]
        mix_b = sum_b / cnt_b - xb[POOL_PAD:POOL_PAD + rows]
        mixed = jnp.concatenate([mix_a, mix_b], axis=1).astype(BF16)
        y = _dot(mixed, w_ref[...]) * sc_ref[...]
        o_ref[0, pl.ds(t0, rows), :] = y.astype(BF16)
        return carry

    lax.fori_loop(0, seq // rows, chunk, 0)


def _pool(p3, w_bd, scale, *, rows=512):
    b, s, c = p3.shape
    return pl.pallas_call(
        functools.partial(_pool_body, seq=s, rows=rows),
        grid=(b,),
        in_specs=[pl.BlockSpec((1, s, c), lambda i: (i, 0, 0)),
                  _const_spec(w_bd.shape),
                  _const_spec((1, c))],
        out_specs=pl.BlockSpec((1, s, c), lambda i: (i, 0, 0)),
        out_shape=jax.ShapeDtypeStruct((b, s, c), BF16),
        scratch_shapes=[pltpu.VMEM((s + 2 * POOL_PAD, c), F32)],
        compiler_params=_params("parallel"),
        name="pool",
    )(p3, w_bd, scale.reshape(1, c))


LRU_PAD = 8
SCAN_ROWS = 8


def _softplus(x):
    return jnp.maximum(x, 0.0) + jnp.log1p(jnp.exp(-jnp.abs(x)))


def _lru_body(lx_ref, lg_ref, cw_ref, cb_ref, wg_ref, bg_ref, lam_ref, o_ref,
              pad_ref, af_ref, uf_ref, ab_ref, ub_ref, *, seq, rows):
    w = LRU_WIDTH
    zeros = jnp.zeros((LRU_PAD, w), F32)
    pad_ref[0:LRU_PAD, :] = zeros
    pad_ref[seq + LRU_PAD:seq + 2 * LRU_PAD, :] = zeros
    pad_ref[LRU_PAD:seq + LRU_PAD, :] = lx_ref[0]

    decay = -LRU_C * _softplus(-lam_ref[...])
    lead = LRU_CONV_WIDTH // 2
    row_in_blk = lax.broadcasted_iota(jnp.int32, (rows, w), 0) % SCAN_ROWS

    def blk_shift(x, d, fill, reverse):
        blocks = x.reshape(rows // SCAN_ROWS, SCAN_ROWS, w)
        if reverse:
            moved = pltpu.roll(blocks, SCAN_ROWS - d, axis=1).reshape(rows, w)
            return jnp.where(row_in_blk < SCAN_ROWS - d, moved, fill)
        moved = pltpu.roll(blocks, d, axis=1).reshape(rows, w)
        return jnp.where(row_in_blk >= d, moved, fill)

    def gates(c, carry):
        t0 = pl.multiple_of(c * rows, rows)
        xs = pad_ref[pl.ds(t0, rows + 2 * LRU_PAD), :]
        xf = cb_ref[...]
        for j in range(LRU_CONV_WIDTH):
            s0 = LRU_PAD - lead + j
            xf = xf + cw_ref[j:j + 1, :] * xs[s0:s0 + rows]
        g = _dot(xf.astype(BF16), wg_ref[...]) + bg_ref[...]
        for d, (a_ref, u_ref) in enumerate(((af_ref, uf_ref), (ab_ref, ub_ref))):
            r = jax.nn.sigmoid(g[:, (2 * d) * w:(2 * d + 1) * w])
            i = jax.nn.sigmoid(g[:, (2 * d + 1) * w:(2 * d + 2) * w])
            log_a = decay[d:d + 1, :] * r
            a = jnp.exp(log_a)
            one_minus_a2 = -jnp.tanh(log_a) * (a * a + 1.0)
            u = jnp.sqrt(one_minus_a2) * (i * xf)
            step = 1
            while step < SCAN_ROWS:
                u = u + a * blk_shift(u, step, 0.0, d == 1)
                a = a * blk_shift(a, step, 1.0, d == 1)
                step *= 2
            a_ref[pl.ds(t0, rows), :] = a
            u_ref[pl.ds(t0, rows), :] = u
        return carry

    lax.fori_loop(0, seq // rows, gates, 0)

    def carry_blocks(i, carry):
        hf, hb = carry
        base_f = pl.multiple_of(i * SCAN_ROWS, SCAN_ROWS)
        base_b = pl.multiple_of(seq - SCAN_ROWS - i * SCAN_ROWS, SCAN_ROWS)
        tf = uf_ref[pl.ds(base_f, SCAN_ROWS), :] + af_ref[pl.ds(base_f, SCAN_ROWS), :] * hf
        uf_ref[pl.ds(base_f, SCAN_ROWS), :] = tf
        tb = ub_ref[pl.ds(base_b, SCAN_ROWS), :] + ab_ref[pl.ds(base_b, SCAN_ROWS), :] * hb
        ub_ref[pl.ds(base_b, SCAN_ROWS), :] = tb
        return tf[SCAN_ROWS - 1:SCAN_ROWS, :], tb[0:1, :]

    h0 = jnp.zeros((1, w), F32)
    lax.fori_loop(0, seq // SCAN_ROWS, carry_blocks, (h0, h0), unroll=8)

    def gate_out(c, carry):
        t0 = pl.multiple_of(c * rows, rows)
        h = uf_ref[pl.ds(t0, rows), :] + ub_ref[pl.ds(t0, rows), :]
        y = jax.nn.gelu(lg_ref[0, pl.ds(t0, rows), :]) * h
        o_ref[0, pl.ds(t0, rows), :] = y.astype(BF16)
        return carry

    lax.fori_loop(0, seq // rows, gate_out, 0)


def _lru(lx3, lg3, conv_w, conv_b, w_gates, b_gates, lam, *, rows=512):
    b, s, c = lx3.shape
    seq_spec = pl.BlockSpec((1, s, c), lambda i: (i, 0, 0))
    return pl.pallas_call(
        functools.partial(_lru_body, seq=s, rows=rows),
        grid=(b,),
        in_specs=[seq_spec, seq_spec,
                  _const_spec(conv_w.shape), _const_spec((1, c)),
                  _const_spec(w_gates.shape), _const_spec((1, 4 * c)),
                  _const_spec(lam.shape)],
        out_specs=seq_spec,
        out_shape=jax.ShapeDtypeStruct((b, s, c), BF16),
        scratch_shapes=[pltpu.VMEM((s + 2 * LRU_PAD, c), F32)]
        + [pltpu.VMEM((s, c), F32) for _ in range(4)],
        compiler_params=_params("parallel"),
        name="lru",
    )(lx3, lg3, conv_w, conv_b.reshape(1, c), w_gates, b_gates.reshape(1, 4 * c), lam)


ALIBI_SPLIT = 64
SKEW = 2
SCORE_LIMIT = 32.0
NORM_MARGIN = 0.98
VALUE_LIMIT = 2.0 ** 80

def _bf16_split3(c):
    pieces = []
    for _ in range(3):
        p = c.astype(BF16).astype(F32)
        pieces.append(p)
        c = c - p
    return pieces


def _alibi_tables(seq):
    assert seq <= ALIBI_SPLIT * 256
    slopes = jnp.asarray([2.0 ** (-8.0 / ATTN_HEADS * (i + 1)) for i in range(ATTN_HEADS)], F32)
    coef = slopes * LOG2E
    pos = jnp.arange(seq)
    parts = [jnp.broadcast_to((ALIBI_SPLIT * (pos // ALIBI_SPLIT)).astype(F32), (ATTN_HEADS, seq)),
             jnp.broadcast_to((pos % ALIBI_SPLIT).astype(F32), (ATTN_HEADS, seq))]
    cs = [jnp.broadcast_to(c[:, None], (ATTN_HEADS, seq)) for c in _bf16_split3(coef)]
    q_cols = [part for part in parts for _ in cs] + [c for _ in parts for c in cs]
    k_cols = [-c for _ in parts for c in cs] + [part for part in parts for _ in cs]
    pad = ((0, 0), (0, 0), (0, LANES - len(q_cols)))
    q_tab = jnp.pad(jnp.stack(q_cols, axis=-1), pad).astype(BF16)
    k_tab = jnp.pad(jnp.stack(k_cols, axis=-1), pad).astype(BF16)
    return coef, q_tab, k_tab


def _attn_body(lam_ref, g_ref, q_ref, qall_ref, qtab_ref, dbias_ref, k_ref, ktab_ref, vt_ref, o_ref,
               acc1_ref, acc2_ref, flag_ref, *, tq, tk, seq, lam_init):
    n_diag = tq // tk
    diag = pl.program_id(2) * n_diag

    qf = q_ref[0].astype(F32)
    lane = lax.broadcasted_iota(jnp.int32, qf.shape, 1)
    q1 = jnp.where(lane < ATTN_QK_DIM, qf, 0.0).astype(BF16)
    q2 = jnp.where(lane >= ATTN_QK_DIM, qf, 0.0).astype(BF16)
    tab_before = qtab_ref[0]
    tab_after = (-tab_before.astype(F32)).astype(BF16)

    same_map = ((lax.broadcasted_iota(jnp.int32, (LANES, LANES), 0) < ATTN_QK_DIM)
                == (lax.broadcasted_iota(jnp.int32, (LANES, LANES), 1) < ATTN_QK_DIM))
    map_ones = jnp.where(same_map, 1.0, 0.0).astype(BF16)

    def max_sq_norm(x):
        return jnp.max(_dot((x * x).astype(BF16), map_ones), axis=0, keepdims=True)

    @pl.when(pl.program_id(2) == 0)
    def _():
        norms = max_sq_norm(qall_ref[0].astype(F32)) * max_sq_norm(k_ref[0].astype(F32))
        v_max = jnp.full((1, LANES), jnp.max(jnp.abs(vt_ref[0].astype(F32))), F32)
        excess = jnp.maximum(norms - SCORE_LIMIT ** 2 * NORM_MARGIN, v_max - VALUE_LIMIT)
        flag_ref[0] = (jnp.max(excess) <= 0.0).astype(jnp.int32)

    plain_softmax_ok = flag_ref[0] == 1

    acc1_ref[...] = jnp.zeros_like(acc1_ref)
    acc2_ref[...] = jnp.zeros_like(acc2_ref)

    hd = ATTN_V_DIM

    def weighted(p, k0):
        pv = _dot(vt_ref[0, :, pl.ds(k0, tk)], p.astype(BF16))
        return jnp.concatenate([pv, jnp.sum(p.reshape(tk // SUBLANES, SUBLANES, tq), axis=0)], axis=0)

    def update(s, m, acc_ref, k0):
        m_new = jnp.maximum(m, jnp.max(s, axis=0, keepdims=True))
        acc_ref[...] = jnp.exp2(m - m_new) * acc_ref[...] + weighted(jnp.exp2(s - m_new), k0)
        return m_new

    def update_plain(s, m, acc_ref, k0):
        acc_ref[...] += weighted(jnp.exp2(s), k0)
        return m

    def folded_scores(jj):
        j = jj + n_diag * (jj >= diag).astype(jnp.int32)
        tab = jnp.where(j > diag, tab_after, tab_before)
        k0 = pl.multiple_of(j * tk, tk)
        ka = jnp.concatenate([k_ref[0, pl.ds(k0, tk), :], ktab_ref[0, pl.ds(k0, tk), :]], axis=1)
        return (_dot_nt(ka, jnp.concatenate([q1, tab], axis=1)),
                _dot_nt(ka, jnp.concatenate([q2, tab], axis=1)), k0)

    def diag_scores(r):
        k0 = pl.multiple_of((diag + r) * tk, tk)
        kc = k_ref[0, pl.ds(k0, tk), :]
        return _dot_nt(kc, q1) + dbias_ref[0, r], _dot_nt(kc, q2) + dbias_ref[0, r], k0

    def all_chunks(step_fn):
        m1 = m2 = jnp.full((1, tq), -jnp.inf, F32)
        pending = []
        for step in range(seq // tk):
            pending.append(diag_scores(step) if step < n_diag else folded_scores(jnp.int32(step - n_diag)))
            if len(pending) > SKEW:
                s1, s2, k0 = pending.pop(0)
                m1 = step_fn(s1, m1, acc1_ref, k0)
                m2 = step_fn(s2, m2, acc2_ref, k0)
        for s1, s2, k0 in pending:
            m1 = step_fn(s1, m1, acc1_ref, k0)
            m2 = step_fn(s2, m2, acc2_ref, k0)

    @pl.when(plain_softmax_ok)
    def _():
        all_chunks(update_plain)

    @pl.when(jnp.logical_not(plain_softmax_ok))
    def _():
        all_chunks(update)

    lp = lam_ref[...]
    lam = (jnp.exp(jnp.sum(lp[0:1] * lp[1:2], axis=-1, keepdims=True))
           - jnp.exp(jnp.sum(lp[2:3] * lp[3:4], axis=-1, keepdims=True)) + lam_init)
    inv1 = 1.0 / jnp.sum(acc1_ref[hd:, :], axis=0, keepdims=True)
    inv2 = lam / jnp.sum(acc2_ref[hd:, :], axis=0, keepdims=True)
    ot = acc1_ref[:hd, :] * inv1 - acc2_ref[:hd, :] * inv2
    ot = ot * lax.rsqrt(jnp.mean(ot * ot, axis=0, keepdims=True) + SUBLN_EPS)
    o_ref[0] = (ot.T * (g_ref[...] * (1.0 - lam_init))).astype(BF16)


def _attn(q3, k3, vt3, lam_params, subln_g, lam_init, *, tq=1024, tk=1024):
    b, s, _ = q3.shape
    hd = ATTN_V_DIM
    assert tq % tk == 0 and s % tq == 0
    n_diag = tq // tk
    coef, q_tab, k_tab = _alibi_tables(s)
    key_pos = (jnp.arange(n_diag)[:, None, None] * tk + jnp.arange(tk)[None, :, None])
    dist = jnp.abs(key_pos - jnp.arange(tq)[None, None, :]).astype(F32)
    diag_bias = -coef[:, None, None, None] * dist
    k_spec = pl.BlockSpec((1, s, hd), lambda bi, hi, qi: (bi, 0, hi))
    vt_spec = pl.BlockSpec((1, hd, s), lambda bi, hi, qi: (bi, hi, 0))
    q_spec = pl.BlockSpec((1, tq, hd), lambda bi, hi, qi: (bi, qi, hi))
    return pl.pallas_call(
        functools.partial(_attn_body, tq=tq, tk=tk, seq=s, lam_init=lam_init),
        grid=(b, ATTN_HEADS, s // tq),
        in_specs=[_const_spec(lam_params.shape),
                  _const_spec((1, hd)),
                  q_spec,
                  k_spec,
                  pl.BlockSpec((1, tq, LANES), lambda bi, hi, qi: (hi, qi, 0)),
                  pl.BlockSpec((1, n_diag, tk, tq), lambda bi, hi, qi: (hi, 0, 0, 0)),
                  k_spec,
                  pl.BlockSpec((1, s, LANES), lambda bi, hi, qi: (hi, 0, 0)),
                  vt_spec],
        out_specs=q_spec,
        out_shape=jax.ShapeDtypeStruct((b, s, ATTN_WIDTH), BF16),
        scratch_shapes=[pltpu.VMEM((hd + SUBLANES, tq), F32), pltpu.VMEM((hd + SUBLANES, tq), F32),
                        pltpu.SMEM((1,), jnp.int32)],
        compiler_params=_params("parallel", "parallel", "arbitrary"),
        name="diff_attn",
    )(lam_params, subln_g.reshape(1, hd), q3, q3, q_tab, diag_bias, k3, k_tab, vt3)


def _merge_body(x_ref, g_ref, wgate_ref, mb_ref, pool_ref, attn_ref, lru_ref,
                wbp_ref, wba_ref, wbl_ref, wout_ref, o_ref):
    x = x_ref[...]
    d = x.shape[-1]
    h = _rms(x, g_ref[...], NORM_EPS).astype(BF16)
    merged = None
    for i, (br_ref, wb_ref) in enumerate(((pool_ref, wbp_ref), (attn_ref, wba_ref), (lru_ref, wbl_ref))):
        logits = _dot(h, wgate_ref[:, i * d:(i + 1) * d]) + mb_ref[i:i + 1, :]
        term = jax.nn.sigmoid(logits) * _dot(br_ref[...], wb_ref[...])
        merged = term if merged is None else merged + term
    o_ref[...] = x + _dot(merged.astype(BF16), wout_ref[...])


def _merge(x2, g, w_gate, merge_bias, pool2, attn2, lru2, wbp, wba, wbl, w_out, *, tm=1024):
    n, d = x2.shape

    def tok(width):
        return pl.BlockSpec((tm, width), lambda i: (i, 0))

    return pl.pallas_call(
        _merge_body,
        grid=(n // tm,),
        in_specs=[tok(d), _const_spec((1, d)), _const_spec(w_gate.shape), _const_spec(merge_bias.shape),
                  tok(pool2.shape[1]), tok(attn2.shape[1]), tok(lru2.shape[1]),
                  _const_spec(wbp.shape), _const_spec(wba.shape), _const_spec(wbl.shape),
                  _const_spec(w_out.shape)],
        out_specs=tok(d),
        out_shape=jax.ShapeDtypeStruct((n, d), F32),
        compiler_params=_params("parallel"),
        name="merge",
    )(x2, g.reshape(1, d), w_gate, merge_bias, pool2, attn2, lru2, wbp, wba, wbl, w_out)


def _block_diag(w):
    g, c, _ = w.shape
    eye = jnp.eye(g, dtype=w.dtype)
    return (eye[:, None, :, None] * w[:, :, None, :]).reshape(g * c, g * c)


def kernel(x, ffn1_norm, ffn1_w_in, ffn1_w_out, mix_norm, w_in, pool_w, pool_scale, attn_lambda, attn_subln, lru_conv_w, lru_conv_b, lru_w_a, lru_b_a, lru_w_x, lru_b_x, lru_lambda, w_branch_pool, w_branch_attn, w_branch_lru, merge_bias, w_out, ffn2_norm, ffn2_w_in, ffn2_w_out, final_norm):
    b, s, d = x.shape
    depth = ffn1_norm.shape[0]
    qk = ATTN_HEADS * 2 * ATTN_QK_DIM
    mix_cols = POOL_WIDTH + 2 * qk + ATTN_WIDTH + 2 * LRU_WIDTH
    x2 = x.reshape(b * s, d)
    for l in range(depth):
        lam_init = 0.8 - 0.6 * math.exp(-0.3 * l)
        last = l == depth - 1
        x2 = _ffn(x2, ffn1_norm[l], ffn1_w_in[l].astype(BF16), ffn1_w_out[l].astype(BF16))

        p, q, k, vt, lx, lg = _inproj(x2, mix_norm[l], w_in[l, :, :mix_cols].astype(BF16), s)
        pool = _pool(p.reshape(b, s, -1), _block_diag(pool_w[l]).astype(BF16), pool_scale[l])
        w_gates = jnp.concatenate(
            [_block_diag(lru_w_a[l, 0]), _block_diag(lru_w_x[l, 0]),
             _block_diag(lru_w_a[l, 1]), _block_diag(lru_w_x[l, 1])], axis=1).astype(BF16)
        b_gates = jnp.concatenate([lru_b_a[l, 0], lru_b_x[l, 0], lru_b_a[l, 1], lru_b_x[l, 1]])
        lru = _lru(lx.reshape(b, s, -1), lg.reshape(b, s, -1), lru_conv_w[l], lru_conv_b[l],
                   w_gates, b_gates, lru_lambda[l])
        attn = _attn(q.reshape(b, s, -1), k.reshape(b, s, -1), vt,
                     attn_lambda[l], attn_subln[l], lam_init)
        x2 = _merge(x2, mix_norm[l], w_in[l, :, mix_cols:].astype(BF16), merge_bias[l],
                    pool.reshape(b * s, -1), attn.reshape(b * s, -1), lru.reshape(b * s, -1),
                    w_branch_pool[l].astype(BF16), w_branch_attn[l].astype(BF16),
                    w_branch_lru[l].astype(BF16), w_out[l].astype(BF16))

        x2 = _ffn(x2, ffn2_norm[l], ffn2_w_in[l].astype(BF16), ffn2_w_out[l].astype(BF16),
                  final_norm if last else None)
    return x2.reshape(b, s, d)
```

```python
import functools
import math

import jax
import jax.numpy as jnp
from jax import lax
from jax.experimental import pallas as pl
from jax.experimental.pallas import tpu as pltpu

F32 = jnp.float32
BF16 = jnp.bfloat16

NORM_EPS = 1e-6
SUBLN_EPS = 1e-5
LRU_C = 8.0

POOL_WINDOWS = (2, 4, 8, 16)
POOL_WIDTH = 256
ATTN_HEADS = 4
ATTN_QK_DIM = 64
ATTN_V_DIM = 128
ATTN_WIDTH = ATTN_HEADS * ATTN_V_DIM
LRU_WIDTH = 256
LRU_CONV_WIDTH = 4

LOG2E = math.log2(math.e)
Q_SCALE = ATTN_QK_DIM ** -0.5 * LOG2E

LANES = 128
SUBLANES = 8
MXU_COLS = 256
VMEM_LIMIT_BYTES = 52 * 1024 * 1024


def _params(*sem):
    return pltpu.CompilerParams(dimension_semantics=sem,
                                vmem_limit_bytes=VMEM_LIMIT_BYTES)


def _const_spec(shape):
    nd = len(shape)
    return pl.BlockSpec(shape, lambda *_: (0,) * nd, pipeline_mode=pl.Buffered(1))


def _rms(x, g, eps):
    ms = jnp.mean(x * x, axis=-1, keepdims=True)
    return x * lax.rsqrt(ms + eps) * g


def _dot(a, b):
    return jnp.dot(a, b, preferred_element_type=F32)


def _dot_nt(a, b):
    return lax.dot_general(a, b, (((1,), (1,)), ((), ())), preferred_element_type=F32)


def _ffn_body(*refs, d_ff, chunk, final):
    if final:
        x_ref, g_ref, win_ref, wout_ref, fg_ref, o_ref, acc_ref = refs
    else:
        x_ref, g_ref, win_ref, wout_ref, o_ref, acc_ref = refs
    x = x_ref[...]
    h = _rms(x, g_ref[...], NORM_EPS).astype(BF16)
    for c in range(d_ff // chunk):
        lo = c * chunk
        gate = _dot(h, win_ref[:, lo:lo + chunk].astype(BF16))
        up = _dot(h, win_ref[:, d_ff + lo:d_ff + lo + chunk].astype(BF16))
        act = (gate * jax.nn.sigmoid(gate) * up).astype(BF16)
        part = _dot(act, wout_ref[lo:lo + chunk, :].astype(BF16))
        if c == 0:
            acc_ref[...] = part
        else:
            acc_ref[...] += part
    y = x + 0.5 * acc_ref[...]
    if final:
        y = _rms(y, fg_ref[...], NORM_EPS)
    o_ref[...] = y


def _layer_spec(stacked, layer):
    zeros = (0,) * (stacked.ndim - 1)
    return pl.BlockSpec((None,) + stacked.shape[1:], lambda *_: (layer,) + zeros,
                        pipeline_mode=pl.Buffered(1))


def _ffn(x2, g, w_in, w_out, layer, final_g=None, *, tm=512):
    n, d = x2.shape
    d_ff = w_out.shape[1]
    final = final_g is not None
    in_specs = [
        pl.BlockSpec((tm, d), lambda i: (i, 0)),
        _const_spec((1, d)),
        _layer_spec(w_in, layer),
        _layer_spec(w_out, layer),
    ]
    args = [x2, g.reshape(1, d), w_in, w_out]
    if final:
        in_specs.append(_const_spec((1, d)))
        args.append(final_g.reshape(1, d))
    return pl.pallas_call(
        functools.partial(_ffn_body, d_ff=d_ff, chunk=MXU_COLS, final=final),
        grid=(n // tm,),
        in_specs=in_specs,
        out_specs=pl.BlockSpec((tm, d), lambda i: (i, 0)),
        out_shape=jax.ShapeDtypeStruct((n, d), F32),
        scratch_shapes=[pltpu.VMEM((tm, d), F32)],
        compiler_params=_params("parallel"),
        name="ffn_final" if final else "ffn",
    )(*args)


def _inproj_body(x_ref, g_ref, w_ref, p_ref, q_ref, k_ref, vt_ref, lx_ref, lg_ref):
    h = _rms(x_ref[...], g_ref[...], NORM_EPS).astype(BF16)
    qk = ATTN_HEADS * 2 * ATTN_QK_DIM
    off = 0

    def proj(width):
        nonlocal off
        out = _dot(h, w_ref[:, off:off + width].astype(BF16))
        off += width
        return out

    p_ref[...] = proj(POOL_WIDTH)
    q_ref[...] = (proj(qk) * Q_SCALE).astype(BF16)
    k_ref[...] = proj(qk).astype(BF16)
    vt_ref[0] = proj(ATTN_WIDTH).T.astype(BF16)
    lx_ref[...] = proj(LRU_WIDTH)
    lg_ref[...] = proj(LRU_WIDTH)


def _inproj(x2, g, w, layer, mix_cols, seq, *, tm=1024):
    n, d = x2.shape
    assert seq % tm == 0
    tiles = seq // tm
    qk = ATTN_HEADS * 2 * ATTN_QK_DIM

    def tok(width, dtype):
        return pl.BlockSpec((tm, width), lambda i: (i, 0)), jax.ShapeDtypeStruct((n, width), dtype)

    vt = (pl.BlockSpec((1, ATTN_WIDTH, tm), lambda i: (i // tiles, 0, i % tiles)),
          jax.ShapeDtypeStruct((n // seq, ATTN_WIDTH, seq), BF16))
    outs = [tok(POOL_WIDTH, F32), tok(qk, BF16), tok(qk, BF16), vt, tok(LRU_WIDTH, F32), tok(LRU_WIDTH, F32)]
    return pl.pallas_call(
        _inproj_body,
        grid=(n // tm,),
        in_specs=[pl.BlockSpec((tm, d), lambda i: (i, 0)),
                  _const_spec((1, d)),
                  pl.BlockSpec((None, d, mix_cols), lambda i: (layer, 0, 0), pipeline_mode=pl.Buffered(1))],
        out_specs=[spec for spec, _ in outs],
        out_shape=[shape for _, shape in outs],
        compiler_params=_params("parallel"),
        name="inproj",
    )(x2, g.reshape(1, d), w)


POOL_PAD = 16


def _pool_body(p_ref, w_ref, sc_ref, o_ref, pad_ref, *, seq, rows):
    zeros = jnp.zeros((POOL_PAD, POOL_WIDTH), F32)
    pad_ref[0:POOL_PAD, :] = zeros
    pad_ref[seq + POOL_PAD:seq + 2 * POOL_PAD, :] = zeros
    pad_ref[POOL_PAD:seq + POOL_PAD, :] = p_ref[0]

    low = lax.broadcasted_iota(jnp.int32, (rows, LANES), 1) < LANES // 2
    row = lax.broadcasted_iota(jnp.int32, (rows, LANES), 0)

    def doubled(a, d):
        return a[d:, :] + a[:a.shape[0] - d, :]

    def chunk(c, carry):
        t0 = pl.multiple_of(c * rows, rows)
        xs = pad_ref[pl.ds(t0, rows + 2 * POOL_PAD), :]
        xa, xb = xs[:, :LANES], xs[:, LANES:]
        a2 = doubled(xa, 1)
        a4 = doubled(a2, 2)
        b8 = doubled(doubled(doubled(xb, 1), 2), 4)
        b16 = doubled(b8, 8)
        sum_a = jnp.where(low, a2[POOL_PAD - 1:POOL_PAD - 1 + rows], a4[POOL_PAD - 2:POOL_PAD - 2 + rows])
        sum_b = jnp.where(low, b8[POOL_PAD - 4:POOL_PAD - 4 + rows], b16[POOL_PAD - 8:POOL_PAD - 8 + rows])
        t = t0 + row

        def count(half):
            return (jnp.minimum(t + half, seq) - jnp.maximum(t - half, 0)).astype(F32)

        cnt_a = jnp.where(low, count(1), count(2))
        cnt_b = jnp.where(low, count(4), count(8))
        mix_a = sum_a / cnt_a - xa[POOL_PAD:POOL_PAD + rows]
        mix_b = sum_b / cnt_b - xb[POOL_PAD:POOL_PAD + rows]
        mixed = jnp.concatenate([mix_a, mix_b], axis=1).astype(BF16)
        y = _dot(mixed, w_ref[...]) * sc_ref[...]
        o_ref[0, pl.ds(t0, rows), :] = y.astype(BF16)
        return carry

    lax.fori_loop(0, seq // rows, chunk, 0)


def _pool(p3, w_bd, scale, *, rows=512):
    b, s, c = p3.shape
    return pl.pallas_call(
        functools.partial(_pool_body, seq=s, rows=rows),
        grid=(b,),
        in_specs=[pl.BlockSpec((1, s, c), lambda i: (i, 0, 0)),
                  _const_spec(w_bd.shape),
                  _const_spec((1, c))],
        out_specs=pl.BlockSpec((1, s, c), lambda i: (i, 0, 0)),
        out_shape=jax.ShapeDtypeStruct((b, s, c), BF16),
        scratch_shapes=[pltpu.VMEM((s + 2 * POOL_PAD, c), F32)],
        compiler_params=_params("parallel"),
        name="pool",
    )(p3, w_bd, scale.reshape(1, c))


LRU_PAD = 8
SCAN_ROWS = 8


def _softplus(x):
    return jnp.maximum(x, 0.0) + jnp.log1p(jnp.exp(-jnp.abs(x)))


def _lru_body(lx_ref, lg_ref, cw_ref, cb_ref, wg_ref, bg_ref, lam_ref, o_ref,
              pad_ref, af_ref, uf_ref, ab_ref, ub_ref, *, seq, rows):
    w = LRU_WIDTH
    zeros = jnp.zeros((LRU_PAD, w), F32)
    pad_ref[0:LRU_PAD, :] = zeros
    pad_ref[seq + LRU_PAD:seq + 2 * LRU_PAD, :] = zeros
    pad_ref[LRU_PAD:seq + LRU_PAD, :] = lx_ref[0]

    decay = -LRU_C * _softplus(-lam_ref[...])
    lead = LRU_CONV_WIDTH // 2
    row_in_blk = lax.broadcasted_iota(jnp.int32, (rows, w), 0) % SCAN_ROWS

    def blk_shift(x, d, fill, reverse):
        blocks = x.reshape(rows // SCAN_ROWS, SCAN_ROWS, w)
        if reverse:
            moved = pltpu.roll(blocks, SCAN_ROWS - d, axis=1).reshape(rows, w)
            return jnp.where(row_in_blk < SCAN_ROWS - d, moved, fill)
        moved = pltpu.roll(blocks, d, axis=1).reshape(rows, w)
        return jnp.where(row_in_blk >= d, moved, fill)

    def gates(c, carry):
        t0 = pl.multiple_of(c * rows, rows)
        xs = pad_ref[pl.ds(t0, rows + 2 * LRU_PAD), :]
        xf = cb_ref[...]
        for j in range(LRU_CONV_WIDTH):
            s0 = LRU_PAD - lead + j
            xf = xf + cw_ref[j:j + 1, :] * xs[s0:s0 + rows]
        g = _dot(xf.astype(BF16), wg_ref[...]) + bg_ref[...]
        for d, (a_ref, u_ref) in enumerate(((af_ref, uf_ref), (ab_ref, ub_ref))):
            r = jax.nn.sigmoid(g[:, (2 * d) * w:(2 * d + 1) * w])
            i = jax.nn.sigmoid(g[:, (2 * d + 1) * w:(2 * d + 2) * w])
            log_a = decay[d:d + 1, :] * r
            a = jnp.exp(log_a)
            one_minus_a2 = -jnp.tanh(log_a) * (a * a + 1.0)
            u = jnp.sqrt(one_minus_a2) * (i * xf)
            step = 1
            while step < SCAN_ROWS:
                u = u + a * blk_shift(u, step, 0.0, d == 1)
                a = a * blk_shift(a, step, 1.0, d == 1)
                step *= 2
            a_ref[pl.ds(t0, rows), :] = a
            u_ref[pl.ds(t0, rows), :] = u
        return carry

    lax.fori_loop(0, seq // rows, gates, 0)

    def carry_blocks(i, carry):
        hf, hb = carry
        base_f = pl.multiple_of(i * SCAN_ROWS, SCAN_ROWS)
        base_b = pl.multiple_of(seq - SCAN_ROWS - i * SCAN_ROWS, SCAN_ROWS)
        tf = uf_ref[pl.ds(base_f, SCAN_ROWS), :] + af_ref[pl.ds(base_f, SCAN_ROWS), :] * hf
        uf_ref[pl.ds(base_f, SCAN_ROWS), :] = tf
        tb = ub_ref[pl.ds(base_b, SCAN_ROWS), :] + ab_ref[pl.ds(base_b, SCAN_ROWS), :] * hb
        ub_ref[pl.ds(base_b, SCAN_ROWS), :] = tb
        return tf[SCAN_ROWS - 1:SCAN_ROWS, :], tb[0:1, :]

    h0 = jnp.zeros((1, w), F32)
    lax.fori_loop(0, seq // SCAN_ROWS, carry_blocks, (h0, h0), unroll=8)

    def gate_out(c, carry):
        t0 = pl.multiple_of(c * rows, rows)
        h = uf_ref[pl.ds(t0, rows), :] + ub_ref[pl.ds(t0, rows), :]
        y = jax.nn.gelu(lg_ref[0, pl.ds(t0, rows), :]) * h
        o_ref[0, pl.ds(t0, rows), :] = y.astype(BF16)
        return carry

    lax.fori_loop(0, seq // rows, gate_out, 0)


def _lru(lx3, lg3, conv_w, conv_b, w_gates, b_gates, lam, *, rows=512):
    b, s, c = lx3.shape
    seq_spec = pl.BlockSpec((1, s, c), lambda i: (i, 0, 0))
    return pl.pallas_call(
        functools.partial(_lru_body, seq=s, rows=rows),
        grid=(b,),
        in_specs=[seq_spec, seq_spec,
                  _const_spec(conv_w.shape), _const_spec((1, c)),
                  _const_spec(w_gates.shape), _const_spec((1, 4 * c)),
                  _const_spec(lam.shape)],
        out_specs=seq_spec,
        out_shape=jax.ShapeDtypeStruct((b, s, c), BF16),
        scratch_shapes=[pltpu.VMEM((s + 2 * LRU_PAD, c), F32)]
        + [pltpu.VMEM((s, c), F32) for _ in range(4)],
        compiler_params=_params("parallel"),
        name="lru",
    )(lx3, lg3, conv_w, conv_b.reshape(1, c), w_gates, b_gates.reshape(1, 4 * c), lam)


ALIBI_SPLIT = 64
SKEW = 2
SCORE_LIMIT = 32.0
NORM_MARGIN = 0.98
VALUE_LIMIT = 2.0 ** 80

def _bf16_split3(c):
    pieces = []
    for _ in range(3):
        p = c.astype(BF16).astype(F32)
        pieces.append(p)
        c = c - p
    return pieces


def _alibi_tables(seq):
    assert seq <= ALIBI_SPLIT * 256
    slopes = jnp.asarray([2.0 ** (-8.0 / ATTN_HEADS * (i + 1)) for i in range(ATTN_HEADS)], F32)
    coef = slopes * LOG2E
    pos = jnp.arange(seq)
    parts = [jnp.broadcast_to((ALIBI_SPLIT * (pos // ALIBI_SPLIT)).astype(F32), (ATTN_HEADS, seq)),
             jnp.broadcast_to((pos % ALIBI_SPLIT).astype(F32), (ATTN_HEADS, seq))]
    cs = [jnp.broadcast_to(c[:, None], (ATTN_HEADS, seq)) for c in _bf16_split3(coef)]
    q_cols = [part for part in parts for _ in cs] + [c for _ in parts for c in cs]
    k_cols = [-c for _ in parts for c in cs] + [part for part in parts for _ in cs]
    pad = ((0, 0), (0, 0), (0, LANES - len(q_cols)))
    q_tab = jnp.pad(jnp.stack(q_cols, axis=-1), pad).astype(BF16)
    k_tab = jnp.pad(jnp.stack(k_cols, axis=-1), pad).astype(BF16)
    return coef, q_tab, k_tab


def _attn_body(lam_ref, g_ref, q_ref, qall_ref, qtab_ref, dbias_ref, k_ref, ktab_ref, vt_ref, o_ref,
               acc1_ref, acc2_ref, flag_ref, *, tq, tk, seq, lam_init):
    n_diag = tq // tk
    diag = pl.program_id(2) * n_diag

    qf = q_ref[0].astype(F32)
    lane = lax.broadcasted_iota(jnp.int32, qf.shape, 1)
    q1 = jnp.where(lane < ATTN_QK_DIM, qf, 0.0).astype(BF16)
    q2 = jnp.where(lane >= ATTN_QK_DIM, qf, 0.0).astype(BF16)
    tab_before = qtab_ref[0]
    tab_after = (-tab_before.astype(F32)).astype(BF16)

    same_map = ((lax.broadcasted_iota(jnp.int32, (LANES, LANES), 0) < ATTN_QK_DIM)
                == (lax.broadcasted_iota(jnp.int32, (LANES, LANES), 1) < ATTN_QK_DIM))
    map_ones = jnp.where(same_map, 1.0, 0.0).astype(BF16)

    def max_sq_norm(x):
        return jnp.max(_dot((x * x).astype(BF16), map_ones), axis=0, keepdims=True)

    @pl.when(pl.program_id(2) == 0)
    def _():
        norms = max_sq_norm(qall_ref[0].astype(F32)) * max_sq_norm(k_ref[0].astype(F32))
        v_max = jnp.full((1, LANES), jnp.max(jnp.abs(vt_ref[0].astype(F32))), F32)
        excess = jnp.maximum(norms - SCORE_LIMIT ** 2 * NORM_MARGIN, v_max - VALUE_LIMIT)
        flag_ref[0] = (jnp.max(excess) <= 0.0).astype(jnp.int32)

    plain_softmax_ok = flag_ref[0] == 1

    acc1_ref[...] = jnp.zeros_like(acc1_ref)
    acc2_ref[...] = jnp.zeros_like(acc2_ref)

    hd = ATTN_V_DIM

    def weighted(p, k0):
        pv = _dot(vt_ref[0, :, pl.ds(k0, tk)], p.astype(BF16))
        return jnp.concatenate([pv, jnp.sum(p.reshape(tk // SUBLANES, SUBLANES, tq), axis=0)], axis=0)

    def update(s, m, acc_ref, k0):
        m_new = jnp.maximum(m, jnp.max(s, axis=0, keepdims=True))
        acc_ref[...] = jnp.exp2(m - m_new) * acc_ref[...] + weighted(jnp.exp2(s - m_new), k0)
        return m_new

    def update_plain(s, m, acc_ref, k0):
        acc_ref[...] += weighted(jnp.exp2(s), k0)
        return m

    def folded_scores(jj):
        j = jj + n_diag * (jj >= diag).astype(jnp.int32)
        tab = jnp.where(j > diag, tab_after, tab_before)
        k0 = pl.multiple_of(j * tk, tk)
        ka = jnp.concatenate([k_ref[0, pl.ds(k0, tk), :], ktab_ref[0, pl.ds(k0, tk), :]], axis=1)
        return (_dot_nt(ka, jnp.concatenate([q1, tab], axis=1)),
                _dot_nt(ka, jnp.concatenate([q2, tab], axis=1)), k0)

    def diag_scores(r):
        k0 = pl.multiple_of((diag + r) * tk, tk)
        kc = k_ref[0, pl.ds(k0, tk), :]
        return _dot_nt(kc, q1) + dbias_ref[0, r], _dot_nt(kc, q2) + dbias_ref[0, r], k0

    def all_chunks(step_fn):
        m1 = m2 = jnp.full((1, tq), -jnp.inf, F32)
        pending = []
        for step in range(seq // tk):
            pending.append(diag_scores(step) if step < n_diag else folded_scores(jnp.int32(step - n_diag)))
            if len(pending) > SKEW:
                s1, s2, k0 = pending.pop(0)
                m1 = step_fn(s1, m1, acc1_ref, k0)
                m2 = step_fn(s2, m2, acc2_ref, k0)
        for s1, s2, k0 in pending:
            m1 = step_fn(s1, m1, acc1_ref, k0)
            m2 = step_fn(s2, m2, acc2_ref, k0)

    @pl.when(plain_softmax_ok)
    def _():
        all_chunks(update_plain)

    @pl.when(jnp.logical_not(plain_softmax_ok))
    def _():
        all_chunks(update)

    lp = lam_ref[...]
    lam = (jnp.exp(jnp.sum(lp[0:1] * lp[1:2], axis=-1, keepdims=True))
           - jnp.exp(jnp.sum(lp[2:3] * lp[3:4], axis=-1, keepdims=True)) + lam_init)
    inv1 = 1.0 / jnp.sum(acc1_ref[hd:, :], axis=0, keepdims=True)
    inv2 = lam / jnp.sum(acc2_ref[hd:, :], axis=0, keepdims=True)
    ot = acc1_ref[:hd, :] * inv1 - acc2_ref[:hd, :] * inv2
    ot = ot * lax.rsqrt(jnp.mean(ot * ot, axis=0, keepdims=True) + SUBLN_EPS)
    o_ref[0] = (ot.T * (g_ref[...] * (1.0 - lam_init))).astype(BF16)


def _attn(q3, k3, vt3, lam_params, subln_g, lam_init, *, tq=1024, tk=1024):
    b, s, _ = q3.shape
    hd = ATTN_V_DIM
    assert tq % tk == 0 and s % tq == 0
    n_diag = tq // tk
    coef, q_tab, k_tab = _alibi_tables(s)
    key_pos = (jnp.arange(n_diag)[:, None, None] * tk + jnp.arange(tk)[None, :, None])
    dist = jnp.abs(key_pos - jnp.arange(tq)[None, None, :]).astype(F32)
    diag_bias = -coef[:, None, None, None] * dist
    k_spec = pl.BlockSpec((1, s, hd), lambda bi, hi, qi: (bi, 0, hi))
    vt_spec = pl.BlockSpec((1, hd, s), lambda bi, hi, qi: (bi, hi, 0))
    q_spec = pl.BlockSpec((1, tq, hd), lambda bi, hi, qi: (bi, qi, hi))
    return pl.pallas_call(
        functools.partial(_attn_body, tq=tq, tk=tk, seq=s, lam_init=lam_init),
        grid=(b, ATTN_HEADS, s // tq),
        in_specs=[_const_spec(lam_params.shape),
                  _const_spec((1, hd)),
                  q_spec,
                  k_spec,
                  pl.BlockSpec((1, tq, LANES), lambda bi, hi, qi: (hi, qi, 0)),
                  pl.BlockSpec((1, n_diag, tk, tq), lambda bi, hi, qi: (hi, 0, 0, 0)),
                  k_spec,
                  pl.BlockSpec((1, s, LANES), lambda bi, hi, qi: (hi, 0, 0)),
                  vt_spec],
        out_specs=q_spec,
        out_shape=jax.ShapeDtypeStruct((b, s, ATTN_WIDTH), BF16),
        scratch_shapes=[pltpu.VMEM((hd + SUBLANES, tq), F32), pltpu.VMEM((hd + SUBLANES, tq), F32),
                        pltpu.SMEM((1,), jnp.int32)],
        compiler_params=_params("parallel", "parallel", "arbitrary"),
        name="diff_attn",
    )(lam_params, subln_g.reshape(1, hd), q3, q3, q_tab, diag_bias, k3, k_tab, vt3)


def _merge_body(*refs, n_gate_blocks):
    (x_ref, g_ref, mb_ref, pool_ref, attn_ref, lru_ref, wbp_ref, wba_ref, wbl_ref, wout_ref), rest = refs[:10], refs[10:]
    wgate_refs, o_ref = rest[:n_gate_blocks], rest[n_gate_blocks]
    x = x_ref[...]
    d = x.shape[-1]
    blk = wgate_refs[0].shape[-1]
    h = _rms(x, g_ref[...], NORM_EPS).astype(BF16)

    def gate_logits(i):
        parts, lo = [], i * d
        while lo < (i + 1) * d:
            off = lo % blk
            width = min(blk - off, (i + 1) * d - lo)
            parts.append(_dot(h, wgate_refs[lo // blk][:, off:off + width].astype(BF16)))
            lo += width
        return jnp.concatenate(parts, axis=1)

    merged = None
    for i, (br_ref, wb_ref) in enumerate(((pool_ref, wbp_ref), (attn_ref, wba_ref), (lru_ref, wbl_ref))):
        logits = gate_logits(i) + mb_ref[i:i + 1, :]
        term = jax.nn.sigmoid(logits) * _dot(br_ref[...], wb_ref[...].astype(BF16))
        merged = term if merged is None else merged + term
    o_ref[...] = x + _dot(merged.astype(BF16), wout_ref[...].astype(BF16))


def _merge(x2, g, w_in, gate_col0, merge_bias, pool2, attn2, lru2, wbp, wba, wbl, w_out, layer, *, tm=1024):
    n, d = x2.shape
    blk = math.gcd(gate_col0, 3 * d)
    assert blk % MXU_COLS == 0
    n_gate_blocks = 3 * d // blk

    def tok(width):
        return pl.BlockSpec((tm, width), lambda i: (i, 0))

    gate_specs = [pl.BlockSpec((None, d, blk), lambda i, j=j: (layer, 0, gate_col0 // blk + j),
                               pipeline_mode=pl.Buffered(1)) for j in range(n_gate_blocks)]
    return pl.pallas_call(
        functools.partial(_merge_body, n_gate_blocks=n_gate_blocks),
        grid=(n // tm,),
        in_specs=[tok(d), _const_spec((1, d)), _const_spec(merge_bias.shape),
                  tok(pool2.shape[1]), tok(attn2.shape[1]), tok(lru2.shape[1]),
                  _layer_spec(wbp, layer), _layer_spec(wba, layer), _layer_spec(wbl, layer),
                  _layer_spec(w_out, layer)] + gate_specs,
        out_specs=tok(d),
        out_shape=jax.ShapeDtypeStruct((n, d), F32),
        compiler_params=_params("parallel"),
        name="merge",
    )(x2, g.reshape(1, d), merge_bias, pool2, attn2, lru2, wbp, wba, wbl, w_out, *([w_in] * n_gate_blocks))


def _block_diag(w):
    g, c, _ = w.shape
    eye = jnp.eye(g, dtype=w.dtype)
    return (eye[:, None, :, None] * w[:, :, None, :]).reshape(g * c, g * c)


def kernel(x, ffn1_norm, ffn1_w_in, ffn1_w_out, mix_norm, w_in, pool_w, pool_scale, attn_lambda, attn_subln, lru_conv_w, lru_conv_b, lru_w_a, lru_b_a, lru_w_x, lru_b_x, lru_lambda, w_branch_pool, w_branch_attn, w_branch_lru, merge_bias, w_out, ffn2_norm, ffn2_w_in, ffn2_w_out, final_norm):
    b, s, d = x.shape
    depth = ffn1_norm.shape[0]
    qk = ATTN_HEADS * 2 * ATTN_QK_DIM
    mix_cols = POOL_WIDTH + 2 * qk + ATTN_WIDTH + 2 * LRU_WIDTH
    x2 = x.reshape(b * s, d)
    for l in range(depth):
        lam_init = 0.8 - 0.6 * math.exp(-0.3 * l)
        last = l == depth - 1
        x2 = _ffn(x2, ffn1_norm[l], ffn1_w_in, ffn1_w_out, l)

        p, q, k, vt, lx, lg = _inproj(x2, mix_norm[l], w_in, l, mix_cols, s)
        pool = _pool(p.reshape(b, s, -1), _block_diag(pool_w[l]).astype(BF16), pool_scale[l])
        w_gates = jnp.concatenate(
            [_block_diag(lru_w_a[l, 0]), _block_diag(lru_w_x[l, 0]),
             _block_diag(lru_w_a[l, 1]), _block_diag(lru_w_x[l, 1])], axis=1).astype(BF16)
        b_gates = jnp.concatenate([lru_b_a[l, 0], lru_b_x[l, 0], lru_b_a[l, 1], lru_b_x[l, 1]])
        lru = _lru(lx.reshape(b, s, -1), lg.reshape(b, s, -1), lru_conv_w[l], lru_conv_b[l],
                   w_gates, b_gates, lru_lambda[l])
        attn = _attn(q.reshape(b, s, -1), k.reshape(b, s, -1), vt,
                     attn_lambda[l], attn_subln[l], lam_init)
        x2 = _merge(x2, mix_norm[l], w_in, mix_cols, merge_bias[l],
                    pool.reshape(b * s, -1), attn.reshape(b * s, -1), lru.reshape(b * s, -1),
                    w_branch_pool, w_branch_attn, w_branch_lru, w_out, l)

        x2 = _ffn(x2, ffn2_norm[l], ffn2_w_in, ffn2_w_out, l, final_norm if last else None)
    return x2.reshape(b, s, d)
```

```python
import functools
import math

import jax
import jax.numpy as jnp
from jax import lax
from jax.experimental import pallas as pl
from jax.experimental.pallas import tpu as pltpu

F32 = jnp.float32
BF16 = jnp.bfloat16

NORM_EPS = 1e-6
SUBLN_EPS = 1e-5
LRU_C = 8.0

POOL_WINDOWS = (2, 4, 8, 16)
POOL_WIDTH = 256
ATTN_HEADS = 4
ATTN_QK_DIM = 64
ATTN_V_DIM = 128
ATTN_WIDTH = ATTN_HEADS * ATTN_V_DIM
LRU_WIDTH = 256
LRU_CONV_WIDTH = 4

LOG2E = math.log2(math.e)
Q_SCALE = ATTN_QK_DIM ** -0.5 * LOG2E

LANES = 128
SUBLANES = 8
MXU_COLS = 256
VMEM_LIMIT_BYTES = 52 * 1024 * 1024
FFN_VMEM_LIMIT_BYTES = 57 * 1024 * 1024


def _params(*sem, vmem_limit_bytes=VMEM_LIMIT_BYTES):
    return pltpu.CompilerParams(dimension_semantics=sem, vmem_limit_bytes=vmem_limit_bytes)


def _const_spec(shape):
    nd = len(shape)
    return pl.BlockSpec(shape, lambda *_: (0,) * nd, pipeline_mode=pl.Buffered(1))


def _rms(x, g, eps):
    ms = jnp.mean(x * x, axis=-1, keepdims=True)
    return x * lax.rsqrt(ms + eps) * g


def _dot(a, b):
    return jnp.dot(a, b, preferred_element_type=F32)


def _dot_nt(a, b):
    return lax.dot_general(a, b, (((1,), (1,)), ((), ())), preferred_element_type=F32)


def _ffn_body(*refs, d_ff, chunk, final):
    if final:
        x_ref, g_ref, win_ref, wout_ref, fg_ref, o_ref, acc_ref = refs
    else:
        x_ref, g_ref, win_ref, wout_ref, o_ref, acc_ref = refs
    x = x_ref[...]
    h = _rms(x, g_ref[...], NORM_EPS).astype(BF16)
    for c in range(d_ff // chunk):
        lo = c * chunk
        gate = _dot(h, win_ref[:, lo:lo + chunk].astype(BF16))
        up = _dot(h, win_ref[:, d_ff + lo:d_ff + lo + chunk].astype(BF16))
        act = (gate * jax.nn.sigmoid(gate) * up).astype(BF16)
        part = _dot(act, wout_ref[lo:lo + chunk, :].astype(BF16))
        if c == 0:
            acc_ref[...] = part
        else:
            acc_ref[...] += part
    y = x + 0.5 * acc_ref[...]
    if final:
        y = _rms(y, fg_ref[...], NORM_EPS)
    o_ref[...] = y


def _layer_spec(stacked, layer):
    zeros = (0,) * (stacked.ndim - 1)
    return pl.BlockSpec((None,) + stacked.shape[1:], lambda *_: (layer,) + zeros,
                        pipeline_mode=pl.Buffered(1))


def _ffn(x2, g, w_in, w_out, layer, final_g=None, *, tm=1024):
    n, d = x2.shape
    d_ff = w_out.shape[1]
    final = final_g is not None
    in_specs = [
        pl.BlockSpec((tm, d), lambda i: (i, 0)),
        _const_spec((1, d)),
        _layer_spec(w_in, layer),
        _layer_spec(w_out, layer),
    ]
    args = [x2, g.reshape(1, d), w_in, w_out]
    if final:
        in_specs.append(_const_spec((1, d)))
        args.append(final_g.reshape(1, d))
    return pl.pallas_call(
        functools.partial(_ffn_body, d_ff=d_ff, chunk=MXU_COLS, final=final),
        grid=(n // tm,),
        in_specs=in_specs,
        out_specs=pl.BlockSpec((tm, d), lambda i: (i, 0)),
        out_shape=jax.ShapeDtypeStruct((n, d), F32),
        scratch_shapes=[pltpu.VMEM((tm, d), F32)],
        compiler_params=_params("parallel", vmem_limit_bytes=FFN_VMEM_LIMIT_BYTES),
        name="ffn_final" if final else "ffn",
    )(*args)


def _inproj_body(x_ref, g_ref, w_ref, p_ref, q_ref, k_ref, vt_ref, lx_ref, lg_ref):
    h = _rms(x_ref[...], g_ref[...], NORM_EPS).astype(BF16)
    qk = ATTN_HEADS * 2 * ATTN_QK_DIM
    off = 0

    def proj(width):
        nonlocal off
        out = _dot(h, w_ref[:, off:off + width].astype(BF16))
        off += width
        return out

    p_ref[...] = proj(POOL_WIDTH)
    q_ref[...] = (proj(qk) * Q_SCALE).astype(BF16)
    k_ref[...] = proj(qk).astype(BF16)
    vt_ref[0] = proj(ATTN_WIDTH).T.astype(BF16)
    lx_ref[...] = proj(LRU_WIDTH)
    lg_ref[...] = proj(LRU_WIDTH)


def _inproj(x2, g, w, layer, mix_cols, seq, *, tm=1024):
    n, d = x2.shape
    assert seq % tm == 0
    tiles = seq // tm
    qk = ATTN_HEADS * 2 * ATTN_QK_DIM

    def tok(width, dtype):
        return pl.BlockSpec((tm, width), lambda i: (i, 0)), jax.ShapeDtypeStruct((n, width), dtype)

    vt = (pl.BlockSpec((1, ATTN_WIDTH, tm), lambda i: (i // tiles, 0, i % tiles)),
          jax.ShapeDtypeStruct((n // seq, ATTN_WIDTH, seq), BF16))
    outs = [tok(POOL_WIDTH, F32), tok(qk, BF16), tok(qk, BF16), vt, tok(LRU_WIDTH, F32), tok(LRU_WIDTH, F32)]
    return pl.pallas_call(
        _inproj_body,
        grid=(n // tm,),
        in_specs=[pl.BlockSpec((tm, d), lambda i: (i, 0)),
                  _const_spec((1, d)),
                  pl.BlockSpec((None, d, mix_cols), lambda i: (layer, 0, 0), pipeline_mode=pl.Buffered(1))],
        out_specs=[spec for spec, _ in outs],
        out_shape=[shape for _, shape in outs],
        compiler_params=_params("parallel"),
        name="inproj",
    )(x2, g.reshape(1, d), w)


POOL_PAD = 16


def _pool_body(p_ref, w_ref, sc_ref, o_ref, pad_ref, *, seq, rows):
    zeros = jnp.zeros((POOL_PAD, POOL_WIDTH), F32)
    pad_ref[0:POOL_PAD, :] = zeros
    pad_ref[seq + POOL_PAD:seq + 2 * POOL_PAD, :] = zeros
    pad_ref[POOL_PAD:seq + POOL_PAD, :] = p_ref[0]

    low = lax.broadcasted_iota(jnp.int32, (rows, LANES), 1) < LANES // 2
    row = lax.broadcasted_iota(jnp.int32, (rows, LANES), 0)

    def doubled(a, d):
        return a[d:, :] + a[:a.shape[0] - d, :]

    def chunk(c, carry):
        t0 = pl.multiple_of(c * rows, rows)
        xs = pad_ref[pl.ds(t0, rows + 2 * POOL_PAD), :]
        xa, xb = xs[:, :LANES], xs[:, LANES:]
        a2 = doubled(xa, 1)
        a4 = doubled(a2, 2)
        b8 = doubled(doubled(doubled(xb, 1), 2), 4)
        b16 = doubled(b8, 8)
        sum_a = jnp.where(low, a2[POOL_PAD - 1:POOL_PAD - 1 + rows], a4[POOL_PAD - 2:POOL_PAD - 2 + rows])
        sum_b = jnp.where(low, b8[POOL_PAD - 4:POOL_PAD - 4 + rows], b16[POOL_PAD - 8:POOL_PAD - 8 + rows])
        t = t0 + row

        def count(half):
            return (jnp.minimum(t + half, seq) - jnp.maximum(t - half, 0)).astype(F32)

        cnt_a = jnp.where(low, count(1), count(2))
        cnt_b = jnp.where(low, count(4), count(8))
        mix_a = sum_a / cnt_a - xa[POOL_PAD:POOL_PAD + rows]
        mix_b = sum_b / cnt_b - xb[POOL_PAD:POOL_PAD + rows]
        mixed = jnp.concatenate([mix_a, mix_b], axis=1).astype(BF16)
        y = _dot(mixed, w_ref[...]) * sc_ref[...]
        o_ref[0, pl.ds(t0, rows), :] = y.astype(BF16)
        return carry

    lax.fori_loop(0, seq // rows, chunk, 0)


def _pool(p3, w_bd, scale, *, rows=512):
    b, s, c = p3.shape
    return pl.pallas_call(
        functools.partial(_pool_body, seq=s, rows=rows),
        grid=(b,),
        in_specs=[pl.BlockSpec((1, s, c), lambda i: (i, 0, 0)),
                  _const_spec(w_bd.shape),
                  _const_spec((1, c))],
        out_specs=pl.BlockSpec((1, s, c), lambda i: (i, 0, 0)),
        out_shape=jax.ShapeDtypeStruct((b, s, c), BF16),
        scratch_shapes=[pltpu.VMEM((s + 2 * POOL_PAD, c), F32)],
        compiler_params=_params("parallel"),
        name="pool",
    )(p3, w_bd, scale.reshape(1, c))


LRU_PAD = 8
SCAN_ROWS = 8


def _softplus(x):
    return jnp.maximum(x, 0.0) + jnp.log1p(jnp.exp(-jnp.abs(x)))


def _lru_body(lx_ref, lg_ref, cw_ref, cb_ref, wg_ref, bg_ref, lam_ref, o_ref,
              pad_ref, af_ref, uf_ref, ab_ref, ub_ref, *, seq, rows):
    w = LRU_WIDTH
    zeros = jnp.zeros((LRU_PAD, w), F32)
    pad_ref[0:LRU_PAD, :] = zeros
    pad_ref[seq + LRU_PAD:seq + 2 * LRU_PAD, :] = zeros
    pad_ref[LRU_PAD:seq + LRU_PAD, :] = lx_ref[0]

    decay = -LRU_C * _softplus(-lam_ref[...])
    lead = LRU_CONV_WIDTH // 2
    row_in_blk = lax.broadcasted_iota(jnp.int32, (rows, w), 0) % SCAN_ROWS

    def blk_shift(x, d, fill, reverse):
        blocks = x.reshape(rows // SCAN_ROWS, SCAN_ROWS, w)
        if reverse:
            moved = pltpu.roll(blocks, SCAN_ROWS - d, axis=1).reshape(rows, w)
            return jnp.where(row_in_blk < SCAN_ROWS - d, moved, fill)
        moved = pltpu.roll(blocks, d, axis=1).reshape(rows, w)
        return jnp.where(row_in_blk >= d, moved, fill)

    def gates(c, carry):
        t0 = pl.multiple_of(c * rows, rows)
        xs = pad_ref[pl.ds(t0, rows + 2 * LRU_PAD), :]
        xf = cb_ref[...]
        for j in range(LRU_CONV_WIDTH):
            s0 = LRU_PAD - lead + j
            xf = xf + cw_ref[j:j + 1, :] * xs[s0:s0 + rows]
        g = _dot(xf.astype(BF16), wg_ref[...]) + bg_ref[...]
        for d, (a_ref, u_ref) in enumerate(((af_ref, uf_ref), (ab_ref, ub_ref))):
            r = jax.nn.sigmoid(g[:, (2 * d) * w:(2 * d + 1) * w])
            i = jax.nn.sigmoid(g[:, (2 * d + 1) * w:(2 * d + 2) * w])
            log_a = decay[d:d + 1, :] * r
            a = jnp.exp(log_a)
            one_minus_a2 = -jnp.tanh(log_a) * (a * a + 1.0)
            u = jnp.sqrt(one_minus_a2) * (i * xf)
            step = 1
            while step < SCAN_ROWS:
                u = u + a * blk_shift(u, step, 0.0, d == 1)
                a = a * blk_shift(a, step, 1.0, d == 1)
                step *= 2
            a_ref[pl.ds(t0, rows), :] = a
            u_ref[pl.ds(t0, rows), :] = u
        return carry

    lax.fori_loop(0, seq // rows, gates, 0)

    def carry_blocks(i, carry):
        hf, hb = carry
        base_f = pl.multiple_of(i * SCAN_ROWS, SCAN_ROWS)
        base_b = pl.multiple_of(seq - SCAN_ROWS - i * SCAN_ROWS, SCAN_ROWS)
        tf = uf_ref[pl.ds(base_f, SCAN_ROWS), :] + af_ref[pl.ds(base_f, SCAN_ROWS), :] * hf
        uf_ref[pl.ds(base_f, SCAN_ROWS), :] = tf
        tb = ub_ref[pl.ds(base_b, SCAN_ROWS), :] + ab_ref[pl.ds(base_b, SCAN_ROWS), :] * hb
        ub_ref[pl.ds(base_b, SCAN_ROWS), :] = tb
        return tf[SCAN_ROWS - 1:SCAN_ROWS, :], tb[0:1, :]

    h0 = jnp.zeros((1, w), F32)
    lax.fori_loop(0, seq // SCAN_ROWS, carry_blocks, (h0, h0), unroll=8)

    def gate_out(c, carry):
        t0 = pl.multiple_of(c * rows, rows)
        h = uf_ref[pl.ds(t0, rows), :] + ub_ref[pl.ds(t0, rows), :]
        y = jax.nn.gelu(lg_ref[0, pl.ds(t0, rows), :]) * h
        o_ref[0, pl.ds(t0, rows), :] = y.astype(BF16)
        return carry

    lax.fori_loop(0, seq // rows, gate_out, 0)


def _lru(lx3, lg3, conv_w, conv_b, w_gates, b_gates, lam, *, rows=512):
    b, s, c = lx3.shape
    seq_spec = pl.BlockSpec((1, s, c), lambda i: (i, 0, 0))
    return pl.pallas_call(
        functools.partial(_lru_body, seq=s, rows=rows),
        grid=(b,),
        in_specs=[seq_spec, seq_spec,
                  _const_spec(conv_w.shape), _const_spec((1, c)),
                  _const_spec(w_gates.shape), _const_spec((1, 4 * c)),
                  _const_spec(lam.shape)],
        out_specs=seq_spec,
        out_shape=jax.ShapeDtypeStruct((b, s, c), BF16),
        scratch_shapes=[pltpu.VMEM((s + 2 * LRU_PAD, c), F32)]
        + [pltpu.VMEM((s, c), F32) for _ in range(4)],
        compiler_params=_params("parallel"),
        name="lru",
    )(lx3, lg3, conv_w, conv_b.reshape(1, c), w_gates, b_gates.reshape(1, 4 * c), lam)


ALIBI_SPLIT = 64
SKEW = 2
SCORE_LIMIT = 32.0
NORM_MARGIN = 0.98
VALUE_LIMIT = 2.0 ** 80

def _bf16_split3(c):
    pieces = []
    for _ in range(3):
        p = c.astype(BF16).astype(F32)
        pieces.append(p)
        c = c - p
    return pieces


def _alibi_tables(seq):
    assert seq <= ALIBI_SPLIT * 256
    slopes = jnp.asarray([2.0 ** (-8.0 / ATTN_HEADS * (i + 1)) for i in range(ATTN_HEADS)], F32)
    coef = slopes * LOG2E
    pos = jnp.arange(seq)
    parts = [jnp.broadcast_to((ALIBI_SPLIT * (pos // ALIBI_SPLIT)).astype(F32), (ATTN_HEADS, seq)),
             jnp.broadcast_to((pos % ALIBI_SPLIT).astype(F32), (ATTN_HEADS, seq))]
    cs = [jnp.broadcast_to(c[:, None], (ATTN_HEADS, seq)) for c in _bf16_split3(coef)]
    q_cols = [part for part in parts for _ in cs] + [c for _ in parts for c in cs]
    k_cols = [-c for _ in parts for c in cs] + [part for part in parts for _ in cs]
    pad = ((0, 0), (0, 0), (0, LANES - len(q_cols)))
    q_tab = jnp.pad(jnp.stack(q_cols, axis=-1), pad).astype(BF16)
    k_tab = jnp.pad(jnp.stack(k_cols, axis=-1), pad).astype(BF16)
    return coef, q_tab, k_tab


def _attn_body(lam_ref, g_ref, q_ref, qall_ref, qtab_ref, dbias_ref, k_ref, ktab_ref, vt_ref, o_ref,
               acc1_ref, acc2_ref, flag_ref, *, tq, tk, seq, lam_init):
    n_diag = tq // tk
    diag = pl.program_id(2) * n_diag

    qf = q_ref[0].astype(F32)
    lane = lax.broadcasted_iota(jnp.int32, qf.shape, 1)
    q1 = jnp.where(lane < ATTN_QK_DIM, qf, 0.0).astype(BF16)
    q2 = jnp.where(lane >= ATTN_QK_DIM, qf, 0.0).astype(BF16)
    tab_before = qtab_ref[0]
    tab_after = (-tab_before.astype(F32)).astype(BF16)

    same_map = ((lax.broadcasted_iota(jnp.int32, (LANES, LANES), 0) < ATTN_QK_DIM)
                == (lax.broadcasted_iota(jnp.int32, (LANES, LANES), 1) < ATTN_QK_DIM))
    map_ones = jnp.where(same_map, 1.0, 0.0).astype(BF16)

    def max_sq_norm(x):
        return jnp.max(_dot((x * x).astype(BF16), map_ones), axis=0, keepdims=True)

    @pl.when(pl.program_id(2) == 0)
    def _():
        norms = max_sq_norm(qall_ref[0].astype(F32)) * max_sq_norm(k_ref[0].astype(F32))
        v_max = jnp.full((1, LANES), jnp.max(jnp.abs(vt_ref[0].astype(F32))), F32)
        excess = jnp.maximum(norms - SCORE_LIMIT ** 2 * NORM_MARGIN, v_max - VALUE_LIMIT)
        flag_ref[0] = (jnp.max(excess) <= 0.0).astype(jnp.int32)

    plain_softmax_ok = flag_ref[0] == 1

    acc1_ref[...] = jnp.zeros_like(acc1_ref)
    acc2_ref[...] = jnp.zeros_like(acc2_ref)

    hd = ATTN_V_DIM

    def weighted(p, k0):
        pv = _dot(vt_ref[0, :, pl.ds(k0, tk)], p.astype(BF16))
        return jnp.concatenate([pv, jnp.sum(p.reshape(tk // SUBLANES, SUBLANES, tq), axis=0)], axis=0)

    def update(s, m, acc_ref, k0):
        m_new = jnp.maximum(m, jnp.max(s, axis=0, keepdims=True))
        acc_ref[...] = jnp.exp2(m - m_new) * acc_ref[...] + weighted(jnp.exp2(s - m_new), k0)
        return m_new

    def update_plain(s, m, acc_ref, k0):
        acc_ref[...] += weighted(jnp.exp2(s), k0)
        return m

    def folded_scores(jj):
        j = jj + n_diag * (jj >= diag).astype(jnp.int32)
        tab = jnp.where(j > diag, tab_after, tab_before)
        k0 = pl.multiple_of(j * tk, tk)
        ka = jnp.concatenate([k_ref[0, pl.ds(k0, tk), :], ktab_ref[0, pl.ds(k0, tk), :]], axis=1)
        return (_dot_nt(ka, jnp.concatenate([q1, tab], axis=1)),
                _dot_nt(ka, jnp.concatenate([q2, tab], axis=1)), k0)

    def diag_scores(r):
        k0 = pl.multiple_of((diag + r) * tk, tk)
        kc = k_ref[0, pl.ds(k0, tk), :]
        return _dot_nt(kc, q1) + dbias_ref[0, r], _dot_nt(kc, q2) + dbias_ref[0, r], k0

    def all_chunks(step_fn):
        m1 = m2 = jnp.full((1, tq), -jnp.inf, F32)
        pending = []
        for step in range(seq // tk):
            pending.append(diag_scores(step) if step < n_diag else folded_scores(jnp.int32(step - n_diag)))
            if len(pending) > SKEW:
                s1, s2, k0 = pending.pop(0)
                m1 = step_fn(s1, m1, acc1_ref, k0)
                m2 = step_fn(s2, m2, acc2_ref, k0)
        for s1, s2, k0 in pending:
            m1 = step_fn(s1, m1, acc1_ref, k0)
            m2 = step_fn(s2, m2, acc2_ref, k0)

    @pl.when(plain_softmax_ok)
    def _():
        all_chunks(update_plain)

    @pl.when(jnp.logical_not(plain_softmax_ok))
    def _():
        all_chunks(update)

    lp = lam_ref[...]
    lam = (jnp.exp(jnp.sum(lp[0:1] * lp[1:2], axis=-1, keepdims=True))
           - jnp.exp(jnp.sum(lp[2:3] * lp[3:4], axis=-1, keepdims=True)) + lam_init)
    inv1 = 1.0 / jnp.sum(acc1_ref[hd:, :], axis=0, keepdims=True)
    inv2 = lam / jnp.sum(acc2_ref[hd:, :], axis=0, keepdims=True)
    ot = acc1_ref[:hd, :] * inv1 - acc2_ref[:hd, :] * inv2
    ot = ot * lax.rsqrt(jnp.mean(ot * ot, axis=0, keepdims=True) + SUBLN_EPS)
    o_ref[0] = (ot.T * (g_ref[...] * (1.0 - lam_init))).astype(BF16)


def _attn(q3, k3, vt3, lam_params, subln_g, lam_init, *, tq=1024, tk=1024):
    b, s, _ = q3.shape
    hd = ATTN_V_DIM
    assert tq % tk == 0 and s % tq == 0
    n_diag = tq // tk
    coef, q_tab, k_tab = _alibi_tables(s)
    key_pos = (jnp.arange(n_diag)[:, None, None] * tk + jnp.arange(tk)[None, :, None])
    dist = jnp.abs(key_pos - jnp.arange(tq)[None, None, :]).astype(F32)
    diag_bias = -coef[:, None, None, None] * dist
    k_spec = pl.BlockSpec((1, s, hd), lambda bi, hi, qi: (bi, 0, hi))
    vt_spec = pl.BlockSpec((1, hd, s), lambda bi, hi, qi: (bi, hi, 0))
    q_spec = pl.BlockSpec((1, tq, hd), lambda bi, hi, qi: (bi, qi, hi))
    return pl.pallas_call(
        functools.partial(_attn_body, tq=tq, tk=tk, seq=s, lam_init=lam_init),
        grid=(b, ATTN_HEADS, s // tq),
        in_specs=[_const_spec(lam_params.shape),
                  _const_spec((1, hd)),
                  q_spec,
                  k_spec,
                  pl.BlockSpec((1, tq, LANES), lambda bi, hi, qi: (hi, qi, 0)),
                  pl.BlockSpec((1, n_diag, tk, tq), lambda bi, hi, qi: (hi, 0, 0, 0)),
                  k_spec,
                  pl.BlockSpec((1, s, LANES), lambda bi, hi, qi: (hi, 0, 0)),
                  vt_spec],
        out_specs=q_spec,
        out_shape=jax.ShapeDtypeStruct((b, s, ATTN_WIDTH), BF16),
        scratch_shapes=[pltpu.VMEM((hd + SUBLANES, tq), F32), pltpu.VMEM((hd + SUBLANES, tq), F32),
                        pltpu.SMEM((1,), jnp.int32)],
        compiler_params=_params("parallel", "parallel", "arbitrary"),
        name="diff_attn",
    )(lam_params, subln_g.reshape(1, hd), q3, q3, q_tab, diag_bias, k3, k_tab, vt3)


def _merge_body(*refs, n_gate_blocks):
    (x_ref, g_ref, mb_ref, pool_ref, attn_ref, lru_ref, wbp_ref, wba_ref, wbl_ref, wout_ref), rest = refs[:10], refs[10:]
    wgate_refs, o_ref = rest[:n_gate_blocks], rest[n_gate_blocks]
    x = x_ref[...]
    d = x.shape[-1]
    blk = wgate_refs[0].shape[-1]
    h = _rms(x, g_ref[...], NORM_EPS).astype(BF16)

    def gate_logits(i):
        parts, lo = [], i * d
        while lo < (i + 1) * d:
            off = lo % blk
            width = min(blk - off, (i + 1) * d - lo)
            parts.append(_dot(h, wgate_refs[lo // blk][:, off:off + width].astype(BF16)))
            lo += width
        return jnp.concatenate(parts, axis=1)

    merged = None
    for i, (br_ref, wb_ref) in enumerate(((pool_ref, wbp_ref), (attn_ref, wba_ref), (lru_ref, wbl_ref))):
        logits = gate_logits(i) + mb_ref[i:i + 1, :]
        term = jax.nn.sigmoid(logits) * _dot(br_ref[...], wb_ref[...].astype(BF16))
        merged = term if merged is None else merged + term
    o_ref[...] = x + _dot(merged.astype(BF16), wout_ref[...].astype(BF16))


def _merge(x2, g, w_in, gate_col0, merge_bias, pool2, attn2, lru2, wbp, wba, wbl, w_out, layer, *, tm=1024):
    n, d = x2.shape
    blk = math.gcd(gate_col0, 3 * d)
    assert blk % MXU_COLS == 0
    n_gate_blocks = 3 * d // blk

    def tok(width):
        return pl.BlockSpec((tm, width), lambda i: (i, 0))

    gate_specs = [pl.BlockSpec((None, d, blk), lambda i, j=j: (layer, 0, gate_col0 // blk + j),
                               pipeline_mode=pl.Buffered(1)) for j in range(n_gate_blocks)]
    return pl.pallas_call(
        functools.partial(_merge_body, n_gate_blocks=n_gate_blocks),
        grid=(n // tm,),
        in_specs=[tok(d), _const_spec((1, d)), _const_spec(merge_bias.shape),
                  tok(pool2.shape[1]), tok(attn2.shape[1]), tok(lru2.shape[1]),
                  _layer_spec(wbp, layer), _layer_spec(wba, layer), _layer_spec(wbl, layer),
                  _layer_spec(w_out, layer)] + gate_specs,
        out_specs=tok(d),
        out_shape=jax.ShapeDtypeStruct((n, d), F32),
        compiler_params=_params("parallel"),
        name="merge",
    )(x2, g.reshape(1, d), merge_bias, pool2, attn2, lru2, wbp, wba, wbl, w_out, *([w_in] * n_gate_blocks))


def _block_diag(w):
    g, c, _ = w.shape
    eye = jnp.eye(g, dtype=w.dtype)
    return (eye[:, None, :, None] * w[:, :, None, :]).reshape(g * c, g * c)


def kernel(x, ffn1_norm, ffn1_w_in, ffn1_w_out, mix_norm, w_in, pool_w, pool_scale, attn_lambda, attn_subln, lru_conv_w, lru_conv_b, lru_w_a, lru_b_a, lru_w_x, lru_b_x, lru_lambda, w_branch_pool, w_branch_attn, w_branch_lru, merge_bias, w_out, ffn2_norm, ffn2_w_in, ffn2_w_out, final_norm):
    b, s, d = x.shape
    depth = ffn1_norm.shape[0]
    qk = ATTN_HEADS * 2 * ATTN_QK_DIM
    mix_cols = POOL_WIDTH + 2 * qk + ATTN_WIDTH + 2 * LRU_WIDTH
    x2 = x.reshape(b * s, d)
    for l in range(depth):
        lam_init = 0.8 - 0.6 * math.exp(-0.3 * l)
        last = l == depth - 1
        x2 = _ffn(x2, ffn1_norm[l], ffn1_w_in, ffn1_w_out, l)

        p, q, k, vt, lx, lg = _inproj(x2, mix_norm[l], w_in, l, mix_cols, s)
        pool = _pool(p.reshape(b, s, -1), _block_diag(pool_w[l]).astype(BF16), pool_scale[l])
        w_gates = jnp.concatenate(
            [_block_diag(lru_w_a[l, 0]), _block_diag(lru_w_x[l, 0]),
             _block_diag(lru_w_a[l, 1]), _block_diag(lru_w_x[l, 1])], axis=1).astype(BF16)
        b_gates = jnp.concatenate([lru_b_a[l, 0], lru_b_x[l, 0], lru_b_a[l, 1], lru_b_x[l, 1]])
        lru = _lru(lx.reshape(b, s, -1), lg.reshape(b, s, -1), lru_conv_w[l], lru_conv_b[l],
                   w_gates, b_gates, lru_lambda[l])
        attn = _attn(q.reshape(b, s, -1), k.reshape(b, s, -1), vt,
                     attn_lambda[l], attn_subln[l], lam_init)
        x2 = _merge(x2, mix_norm[l], w_in, mix_cols, merge_bias[l],
                    pool.reshape(b * s, -1), attn.reshape(b * s, -1), lru.reshape(b * s, -1),
                    w_branch_pool, w_branch_attn, w_branch_lru, w_out, l)

        x2 = _ffn(x2, ffn2_norm[l], ffn2_w_in, ffn2_w_out, l, final_norm if last else None)
    return x2.reshape(b, s, d)
```

```python
import functools
import math

import jax
import jax.numpy as jnp
from jax import lax
from jax.experimental import pallas as pl
from jax.experimental.pallas import tpu as pltpu

F32 = jnp.float32
BF16 = jnp.bfloat16

NORM_EPS = 1e-6
SUBLN_EPS = 1e-5
LRU_C = 8.0

POOL_WIDTH = 256
ATTN_HEADS = 4
ATTN_QK_DIM = 64
ATTN_V_DIM = 128
ATTN_WIDTH = ATTN_HEADS * ATTN_V_DIM
LRU_WIDTH = 256
LRU_CONV_WIDTH = 4

LOG2E = math.log2(math.e)
Q_SCALE = ATTN_QK_DIM ** -0.5 * LOG2E

LANES = 128
SUBLANES = 8
MXU_COLS = 256
VMEM_LIMIT_BYTES = 52 * 1024 * 1024
FFN_VMEM_LIMIT_BYTES = 57 * 1024 * 1024


def _params(*sem, vmem_limit_bytes=VMEM_LIMIT_BYTES):
    return pltpu.CompilerParams(dimension_semantics=sem, vmem_limit_bytes=vmem_limit_bytes)


def _const_spec(shape):
    nd = len(shape)
    return pl.BlockSpec(shape, lambda *_: (0,) * nd, pipeline_mode=pl.Buffered(1))


def _rms(x, g, eps):
    ms = jnp.mean(x * x, axis=-1, keepdims=True)
    return x * lax.rsqrt(ms + eps) * g


def _dot(a, b):
    return jnp.dot(a, b, preferred_element_type=F32)


def _dot_nt(a, b):
    return lax.dot_general(a, b, (((1,), (1,)), ((), ())), preferred_element_type=F32)


def _ffn_body(*refs, d_ff, chunk, final):
    if final:
        x_ref, g_ref, win_ref, wout_ref, fg_ref, o_ref, acc_ref = refs
    else:
        x_ref, g_ref, win_ref, wout_ref, o_ref, acc_ref = refs
    x = x_ref[...]
    h = _rms(x, g_ref[...], NORM_EPS).astype(BF16)
    for c in range(d_ff // chunk):
        lo = c * chunk
        gate = _dot(h, win_ref[:, lo:lo + chunk].astype(BF16))
        up = _dot(h, win_ref[:, d_ff + lo:d_ff + lo + chunk].astype(BF16))
        act = (gate * jax.nn.sigmoid(gate) * up).astype(BF16)
        part = _dot(act, wout_ref[lo:lo + chunk, :].astype(BF16))
        if c == 0:
            acc_ref[...] = part
        else:
            acc_ref[...] += part
    y = x + 0.5 * acc_ref[...]
    if final:
        y = _rms(y, fg_ref[...], NORM_EPS)
    o_ref[...] = y


def _layer_spec(stacked, layer):
    zeros = (0,) * (stacked.ndim - 1)
    return pl.BlockSpec((None,) + stacked.shape[1:], lambda *_: (layer,) + zeros,
                        pipeline_mode=pl.Buffered(1))


def _ffn(x2, g, w_in, w_out, layer, final_g=None, *, tm=1024):
    n, d = x2.shape
    d_ff = w_out.shape[1]
    assert n % tm == 0 and d_ff % MXU_COLS == 0
    final = final_g is not None
    in_specs = [
        pl.BlockSpec((tm, d), lambda i: (i, 0)),
        _const_spec((1, d)),
        _layer_spec(w_in, layer),
        _layer_spec(w_out, layer),
    ]
    args = [x2, g.reshape(1, d), w_in, w_out]
    if final:
        in_specs.append(_const_spec((1, d)))
        args.append(final_g.reshape(1, d))
    return pl.pallas_call(
        functools.partial(_ffn_body, d_ff=d_ff, chunk=MXU_COLS, final=final),
        grid=(n // tm,),
        in_specs=in_specs,
        out_specs=pl.BlockSpec((tm, d), lambda i: (i, 0)),
        out_shape=jax.ShapeDtypeStruct((n, d), F32),
        scratch_shapes=[pltpu.VMEM((tm, d), F32)],
        compiler_params=_params("parallel", vmem_limit_bytes=FFN_VMEM_LIMIT_BYTES),
        name="ffn_final" if final else "ffn",
    )(*args)


def _inproj_body(x_ref, g_ref, w_ref, p_ref, q_ref, k_ref, vt_ref, lx_ref, lg_ref):
    h = _rms(x_ref[...], g_ref[...], NORM_EPS).astype(BF16)
    qk = ATTN_HEADS * 2 * ATTN_QK_DIM
    off = 0

    def proj(width):
        nonlocal off
        out = _dot(h, w_ref[:, off:off + width].astype(BF16))
        off += width
        return out

    p_ref[...] = proj(POOL_WIDTH)
    q_ref[...] = (proj(qk) * Q_SCALE).astype(BF16)
    k_ref[...] = proj(qk).astype(BF16)
    vt_ref[0] = proj(ATTN_WIDTH).T.astype(BF16)
    lx_ref[...] = proj(LRU_WIDTH)
    lg_ref[...] = proj(LRU_WIDTH)


def _inproj(x2, g, w, layer, mix_cols, seq, *, tm=1024):
    n, d = x2.shape
    assert seq % tm == 0 and n % seq == 0
    tiles = seq // tm
    qk = ATTN_HEADS * 2 * ATTN_QK_DIM

    def tok(width, dtype):
        return pl.BlockSpec((tm, width), lambda i: (i, 0)), jax.ShapeDtypeStruct((n, width), dtype)

    vt = (pl.BlockSpec((1, ATTN_WIDTH, tm), lambda i: (i // tiles, 0, i % tiles)),
          jax.ShapeDtypeStruct((n // seq, ATTN_WIDTH, seq), BF16))
    outs = [tok(POOL_WIDTH, F32), tok(qk, BF16), tok(qk, BF16), vt, tok(LRU_WIDTH, F32), tok(LRU_WIDTH, F32)]
    return pl.pallas_call(
        _inproj_body,
        grid=(n // tm,),
        in_specs=[pl.BlockSpec((tm, d), lambda i: (i, 0)),
                  _const_spec((1, d)),
                  pl.BlockSpec((None, d, mix_cols), lambda i: (layer, 0, 0), pipeline_mode=pl.Buffered(1))],
        out_specs=[spec for spec, _ in outs],
        out_shape=[shape for _, shape in outs],
        compiler_params=_params("parallel"),
        name="inproj",
    )(x2, g.reshape(1, d), w)


POOL_PAD = 16


def _pool_body(p_ref, w_ref, sc_ref, o_ref, pad_ref, *, seq, rows):
    zeros = jnp.zeros((POOL_PAD, POOL_WIDTH), F32)
    pad_ref[0:POOL_PAD, :] = zeros
    pad_ref[seq + POOL_PAD:seq + 2 * POOL_PAD, :] = zeros
    pad_ref[POOL_PAD:seq + POOL_PAD, :] = p_ref[0]

    low = lax.broadcasted_iota(jnp.int32, (rows, LANES), 1) < LANES // 2
    row = lax.broadcasted_iota(jnp.int32, (rows, LANES), 0)

    def doubled(a, d):
        return a[d:, :] + a[:a.shape[0] - d, :]

    def chunk(c, carry):
        t0 = pl.multiple_of(c * rows, rows)
        xs = pad_ref[pl.ds(t0, rows + 2 * POOL_PAD), :]
        xa, xb = xs[:, :LANES], xs[:, LANES:]
        a2 = doubled(xa, 1)
        a4 = doubled(a2, 2)
        b8 = doubled(doubled(doubled(xb, 1), 2), 4)
        b16 = doubled(b8, 8)
        sum_a = jnp.where(low, a2[POOL_PAD - 1:POOL_PAD - 1 + rows], a4[POOL_PAD - 2:POOL_PAD - 2 + rows])
        sum_b = jnp.where(low, b8[POOL_PAD - 4:POOL_PAD - 4 + rows], b16[POOL_PAD - 8:POOL_PAD - 8 + rows])
        t = t0 + row

        def count(half):
            return (jnp.minimum(t + half, seq) - jnp.maximum(t - half, 0)).astype(F32)

        cnt_a = jnp.where(low, count(1), count(2))
        cnt_b = jnp.where(low, count(4), count(8))
        mix_a = sum_a / cnt_a - xa[POOL_PAD:POOL_PAD + rows]
        mix_b = sum_b / cnt_b - xb[POOL_PAD:POOL_PAD + rows]
        mixed = jnp.concatenate([mix_a, mix_b], axis=1).astype(BF16)
        y = _dot(mixed, w_ref[...]) * sc_ref[...]
        o_ref[0, pl.ds(t0, rows), :] = y.astype(BF16)
        return carry

    lax.fori_loop(0, seq // rows, chunk, 0)


def _pool(p3, w_bd, scale, *, rows=512):
    b, s, c = p3.shape
    assert s % rows == 0 and c == POOL_WIDTH == 2 * LANES
    return pl.pallas_call(
        functools.partial(_pool_body, seq=s, rows=rows),
        grid=(b,),
        in_specs=[pl.BlockSpec((1, s, c), lambda i: (i, 0, 0)),
                  _const_spec(w_bd.shape),
                  _const_spec((1, c))],
        out_specs=pl.BlockSpec((1, s, c), lambda i: (i, 0, 0)),
        out_shape=jax.ShapeDtypeStruct((b, s, c), BF16),
        scratch_shapes=[pltpu.VMEM((s + 2 * POOL_PAD, c), F32)],
        compiler_params=_params("parallel"),
        name="pool",
    )(p3, w_bd, scale.reshape(1, c))


LRU_PAD = 8
SCAN_ROWS = 8


def _softplus(x):
    return jnp.maximum(x, 0.0) + jnp.log1p(jnp.exp(-jnp.abs(x)))


def _lru_body(lx_ref, lg_ref, cw_ref, cb_ref, wg_ref, bg_ref, lam_ref, o_ref,
              pad_ref, af_ref, uf_ref, ab_ref, ub_ref, *, seq, rows):
    w = LRU_WIDTH
    zeros = jnp.zeros((LRU_PAD, w), F32)
    pad_ref[0:LRU_PAD, :] = zeros
    pad_ref[seq + LRU_PAD:seq + 2 * LRU_PAD, :] = zeros
    pad_ref[LRU_PAD:seq + LRU_PAD, :] = lx_ref[0]

    decay = -LRU_C * _softplus(-lam_ref[...])
    lead = LRU_CONV_WIDTH // 2
    row_in_blk = lax.broadcasted_iota(jnp.int32, (rows, w), 0) % SCAN_ROWS

    def blk_shift(x, d, fill, reverse):
        blocks = x.reshape(rows // SCAN_ROWS, SCAN_ROWS, w)
        if reverse:
            moved = pltpu.roll(blocks, SCAN_ROWS - d, axis=1).reshape(rows, w)
            return jnp.where(row_in_blk < SCAN_ROWS - d, moved, fill)
        moved = pltpu.roll(blocks, d, axis=1).reshape(rows, w)
        return jnp.where(row_in_blk >= d, moved, fill)

    def gates(c, carry):
        t0 = pl.multiple_of(c * rows, rows)
        xs = pad_ref[pl.ds(t0, rows + 2 * LRU_PAD), :]
        xf = cb_ref[...]
        for j in range(LRU_CONV_WIDTH):
            s0 = LRU_PAD - lead + j
            xf = xf + cw_ref[j:j + 1, :] * xs[s0:s0 + rows]
        g = _dot(xf.astype(BF16), wg_ref[...]) + bg_ref[...]
        for d, (a_ref, u_ref) in enumerate(((af_ref, uf_ref), (ab_ref, ub_ref))):
            r = jax.nn.sigmoid(g[:, (2 * d) * w:(2 * d + 1) * w])
            i = jax.nn.sigmoid(g[:, (2 * d + 1) * w:(2 * d + 2) * w])
            log_a = decay[d:d + 1, :] * r
            a = jnp.exp(log_a)
            one_minus_a2 = -jnp.tanh(log_a) * (a * a + 1.0)
            u = jnp.sqrt(one_minus_a2) * (i * xf)
            step = 1
            while step < SCAN_ROWS:
                u = u + a * blk_shift(u, step, 0.0, d == 1)
                a = a * blk_shift(a, step, 1.0, d == 1)
                step *= 2
            a_ref[pl.ds(t0, rows), :] = a
            u_ref[pl.ds(t0, rows), :] = u
        return carry

    lax.fori_loop(0, seq // rows, gates, 0)

    def carry_blocks(i, carry):
        hf, hb = carry
        base_f = pl.multiple_of(i * SCAN_ROWS, SCAN_ROWS)
        base_b = pl.multiple_of(seq - SCAN_ROWS - i * SCAN_ROWS, SCAN_ROWS)
        tf = uf_ref[pl.ds(base_f, SCAN_ROWS), :] + af_ref[pl.ds(base_f, SCAN_ROWS), :] * hf
        uf_ref[pl.ds(base_f, SCAN_ROWS), :] = tf
        tb = ub_ref[pl.ds(base_b, SCAN_ROWS), :] + ab_ref[pl.ds(base_b, SCAN_ROWS), :] * hb
        ub_ref[pl.ds(base_b, SCAN_ROWS), :] = tb
        return tf[SCAN_ROWS - 1:SCAN_ROWS, :], tb[0:1, :]

    h0 = jnp.zeros((1, w), F32)
    lax.fori_loop(0, seq // SCAN_ROWS, carry_blocks, (h0, h0), unroll=8)

    def gate_out(c, carry):
        t0 = pl.multiple_of(c * rows, rows)
        h = uf_ref[pl.ds(t0, rows), :] + ub_ref[pl.ds(t0, rows), :]
        y = jax.nn.gelu(lg_ref[0, pl.ds(t0, rows), :]) * h
        o_ref[0, pl.ds(t0, rows), :] = y.astype(BF16)
        return carry

    lax.fori_loop(0, seq // rows, gate_out, 0)


def _lru(lx3, lg3, conv_w, conv_b, w_gates, b_gates, lam, *, rows=512):
    b, s, c = lx3.shape
    assert s % rows == 0 and rows % SCAN_ROWS == 0 and c == LRU_WIDTH
    seq_spec = pl.BlockSpec((1, s, c), lambda i: (i, 0, 0))
    return pl.pallas_call(
        functools.partial(_lru_body, seq=s, rows=rows),
        grid=(b,),
        in_specs=[seq_spec, seq_spec,
                  _const_spec(conv_w.shape), _const_spec((1, c)),
                  _const_spec(w_gates.shape), _const_spec((1, 4 * c)),
                  _const_spec(lam.shape)],
        out_specs=seq_spec,
        out_shape=jax.ShapeDtypeStruct((b, s, c), BF16),
        scratch_shapes=[pltpu.VMEM((s + 2 * LRU_PAD, c), F32)]
        + [pltpu.VMEM((s, c), F32) for _ in range(4)],
        compiler_params=_params("parallel"),
        name="lru",
    )(lx3, lg3, conv_w, conv_b.reshape(1, c), w_gates, b_gates.reshape(1, 4 * c), lam)


ALIBI_SPLIT = 64
SKEW = 2
SCORE_LIMIT = 32.0
NORM_MARGIN = 0.98
VALUE_LIMIT = 2.0 ** 80

def _bf16_split3(c):
    pieces = []
    for _ in range(3):
        p = c.astype(BF16).astype(F32)
        pieces.append(p)
        c = c - p
    return pieces


def _alibi_tables(seq):
    assert seq <= ALIBI_SPLIT * 256
    slopes = jnp.asarray([2.0 ** (-8.0 / ATTN_HEADS * (i + 1)) for i in range(ATTN_HEADS)], F32)
    coef = slopes * LOG2E
    pos = jnp.arange(seq)
    parts = [jnp.broadcast_to((ALIBI_SPLIT * (pos // ALIBI_SPLIT)).astype(F32), (ATTN_HEADS, seq)),
             jnp.broadcast_to((pos % ALIBI_SPLIT).astype(F32), (ATTN_HEADS, seq))]
    cs = [jnp.broadcast_to(c[:, None], (ATTN_HEADS, seq)) for c in _bf16_split3(coef)]
    q_cols = [part for part in parts for _ in cs] + [c for _ in parts for c in cs]
    k_cols = [-c for _ in parts for c in cs] + [part for part in parts for _ in cs]
    pad = ((0, 0), (0, 0), (0, LANES - len(q_cols)))
    q_tab = jnp.pad(jnp.stack(q_cols, axis=-1), pad).astype(BF16)
    k_tab = jnp.pad(jnp.stack(k_cols, axis=-1), pad).astype(BF16)
    return coef, q_tab, k_tab


def _attn_body(lam_ref, g_ref, q_ref, qall_ref, qtab_ref, dbias_ref, k_ref, ktab_ref, vt_ref, o_ref,
               acc1_ref, acc2_ref, flag_ref, *, tq, tk, seq, lam_init):
    n_diag = tq // tk
    diag = pl.program_id(2) * n_diag

    qf = q_ref[0].astype(F32)
    lane = lax.broadcasted_iota(jnp.int32, qf.shape, 1)
    q1 = jnp.where(lane < ATTN_QK_DIM, qf, 0.0).astype(BF16)
    q2 = jnp.where(lane >= ATTN_QK_DIM, qf, 0.0).astype(BF16)
    tab_before = qtab_ref[0]
    tab_after = (-tab_before.astype(F32)).astype(BF16)

    same_map = ((lax.broadcasted_iota(jnp.int32, (LANES, LANES), 0) < ATTN_QK_DIM)
                == (lax.broadcasted_iota(jnp.int32, (LANES, LANES), 1) < ATTN_QK_DIM))
    map_ones = jnp.where(same_map, 1.0, 0.0).astype(BF16)

    def max_sq_norm(x):
        return jnp.max(_dot((x * x).astype(BF16), map_ones), axis=0, keepdims=True)

    @pl.when(pl.program_id(2) == 0)
    def _():
        norms = max_sq_norm(qall_ref[0].astype(F32)) * max_sq_norm(k_ref[0].astype(F32))
        v_max = jnp.full((1, LANES), jnp.max(jnp.abs(vt_ref[0].astype(F32))), F32)
        excess = jnp.maximum(norms - SCORE_LIMIT ** 2 * NORM_MARGIN, v_max - VALUE_LIMIT)
        flag_ref[0] = (jnp.max(excess) <= 0.0).astype(jnp.int32)

    plain_softmax_ok = flag_ref[0] == 1

    acc1_ref[...] = jnp.zeros_like(acc1_ref)
    acc2_ref[...] = jnp.zeros_like(acc2_ref)

    hd = ATTN_V_DIM

    def weighted(p, k0):
        pv = _dot(vt_ref[0, :, pl.ds(k0, tk)], p.astype(BF16))
        return jnp.concatenate([pv, jnp.sum(p.reshape(tk // SUBLANES, SUBLANES, tq), axis=0)], axis=0)

    def update(s, m, acc_ref, k0):
        m_new = jnp.maximum(m, jnp.max(s, axis=0, keepdims=True))
        acc_ref[...] = jnp.exp2(m - m_new) * acc_ref[...] + weighted(jnp.exp2(s - m_new), k0)
        return m_new

    def update_plain(s, m, acc_ref, k0):
        acc_ref[...] += weighted(jnp.exp2(s), k0)
        return m

    def folded_scores(jj):
        j = jj + n_diag * (jj >= diag).astype(jnp.int32)
        tab = jnp.where(j > diag, tab_after, tab_before)
        k0 = pl.multiple_of(j * tk, tk)
        ka = jnp.concatenate([k_ref[0, pl.ds(k0, tk), :], ktab_ref[0, pl.ds(k0, tk), :]], axis=1)
        return (_dot_nt(ka, jnp.concatenate([q1, tab], axis=1)),
                _dot_nt(ka, jnp.concatenate([q2, tab], axis=1)), k0)

    def diag_scores(r):
        k0 = pl.multiple_of((diag + r) * tk, tk)
        kc = k_ref[0, pl.ds(k0, tk), :]
        return _dot_nt(kc, q1) + dbias_ref[0, r], _dot_nt(kc, q2) + dbias_ref[0, r], k0

    def all_chunks(step_fn):
        m1 = m2 = jnp.full((1, tq), -jnp.inf, F32)
        pending = []
        for step in range(seq // tk):
            pending.append(diag_scores(step) if step < n_diag else folded_scores(jnp.int32(step - n_diag)))
            if len(pending) > SKEW:
                s1, s2, k0 = pending.pop(0)
                m1 = step_fn(s1, m1, acc1_ref, k0)
                m2 = step_fn(s2, m2, acc2_ref, k0)
        for s1, s2, k0 in pending:
            m1 = step_fn(s1, m1, acc1_ref, k0)
            m2 = step_fn(s2, m2, acc2_ref, k0)

    @pl.when(plain_softmax_ok)
    def _():
        all_chunks(update_plain)

    @pl.when(jnp.logical_not(plain_softmax_ok))
    def _():
        all_chunks(update)

    lp = lam_ref[...]
    lam = (jnp.exp(jnp.sum(lp[0:1] * lp[1:2], axis=-1, keepdims=True))
           - jnp.exp(jnp.sum(lp[2:3] * lp[3:4], axis=-1, keepdims=True)) + lam_init)
    inv1 = 1.0 / jnp.sum(acc1_ref[hd:, :], axis=0, keepdims=True)
    inv2 = lam / jnp.sum(acc2_ref[hd:, :], axis=0, keepdims=True)
    ot = acc1_ref[:hd, :] * inv1 - acc2_ref[:hd, :] * inv2
    ot = ot * lax.rsqrt(jnp.mean(ot * ot, axis=0, keepdims=True) + SUBLN_EPS)
    o_ref[0] = (ot.T * (g_ref[...] * (1.0 - lam_init))).astype(BF16)


def _attn(q3, k3, vt3, lam_params, subln_g, lam_init, *, tq=1024, tk=1024):
    b, s, _ = q3.shape
    hd = ATTN_V_DIM
    assert tq % tk == 0 and s % tq == 0 and hd == LANES == 2 * ATTN_QK_DIM
    n_diag = tq // tk
    coef, q_tab, k_tab = _alibi_tables(s)
    key_pos = (jnp.arange(n_diag)[:, None, None] * tk + jnp.arange(tk)[None, :, None])
    dist = jnp.abs(key_pos - jnp.arange(tq)[None, None, :]).astype(F32)
    diag_bias = -coef[:, None, None, None] * dist
    k_spec = pl.BlockSpec((1, s, hd), lambda bi, hi, qi: (bi, 0, hi))
    vt_spec = pl.BlockSpec((1, hd, s), lambda bi, hi, qi: (bi, hi, 0))
    q_spec = pl.BlockSpec((1, tq, hd), lambda bi, hi, qi: (bi, qi, hi))
    return pl.pallas_call(
        functools.partial(_attn_body, tq=tq, tk=tk, seq=s, lam_init=lam_init),
        grid=(b, ATTN_HEADS, s // tq),
        in_specs=[_const_spec(lam_params.shape),
                  _const_spec((1, hd)),
                  q_spec,
                  k_spec,
                  pl.BlockSpec((1, tq, LANES), lambda bi, hi, qi: (hi, qi, 0)),
                  pl.BlockSpec((1, n_diag, tk, tq), lambda bi, hi, qi: (hi, 0, 0, 0)),
                  k_spec,
                  pl.BlockSpec((1, s, LANES), lambda bi, hi, qi: (hi, 0, 0)),
                  vt_spec],
        out_specs=q_spec,
        out_shape=jax.ShapeDtypeStruct((b, s, ATTN_WIDTH), BF16),
        scratch_shapes=[pltpu.VMEM((hd + SUBLANES, tq), F32), pltpu.VMEM((hd + SUBLANES, tq), F32),
                        pltpu.SMEM((1,), jnp.int32)],
        compiler_params=_params("parallel", "parallel", "arbitrary"),
        name="diff_attn",
    )(lam_params, subln_g.reshape(1, hd), q3, q3, q_tab, diag_bias, k3, k_tab, vt3)


def _merge_body(*refs, n_gate_blocks):
    (x_ref, g_ref, mb_ref, pool_ref, attn_ref, lru_ref, wbp_ref, wba_ref, wbl_ref, wout_ref), rest = refs[:10], refs[10:]
    wgate_refs, o_ref = rest[:n_gate_blocks], rest[n_gate_blocks]
    x = x_ref[...]
    d = x.shape[-1]
    blk = wgate_refs[0].shape[-1]
    h = _rms(x, g_ref[...], NORM_EPS).astype(BF16)

    def gate_logits(i):
        parts, lo = [], i * d
        while lo < (i + 1) * d:
            off = lo % blk
            width = min(blk - off, (i + 1) * d - lo)
            parts.append(_dot(h, wgate_refs[lo // blk][:, off:off + width].astype(BF16)))
            lo += width
        return jnp.concatenate(parts, axis=1)

    merged = None
    for i, (br_ref, wb_ref) in enumerate(((pool_ref, wbp_ref), (attn_ref, wba_ref), (lru_ref, wbl_ref))):
        logits = gate_logits(i) + mb_ref[i:i + 1, :]
        term = jax.nn.sigmoid(logits) * _dot(br_ref[...], wb_ref[...].astype(BF16))
        merged = term if merged is None else merged + term
    o_ref[...] = x + _dot(merged.astype(BF16), wout_ref[...].astype(BF16))


def _merge(x2, g, w_in, gate_col0, merge_bias, pool2, attn2, lru2, wbp, wba, wbl, w_out, layer, *, tm=1024):
    n, d = x2.shape
    blk = math.gcd(gate_col0, 3 * d)
    assert blk % MXU_COLS == 0 and n % tm == 0
    n_gate_blocks = 3 * d // blk

    def tok(width):
        return pl.BlockSpec((tm, width), lambda i: (i, 0))

    gate_specs = [pl.BlockSpec((None, d, blk), lambda i, j=j: (layer, 0, gate_col0 // blk + j),
                               pipeline_mode=pl.Buffered(1)) for j in range(n_gate_blocks)]
    return pl.pallas_call(
        functools.partial(_merge_body, n_gate_blocks=n_gate_blocks),
        grid=(n // tm,),
        in_specs=[tok(d), _const_spec((1, d)), _const_spec(merge_bias.shape),
                  tok(pool2.shape[1]), tok(attn2.shape[1]), tok(lru2.shape[1]),
                  _layer_spec(wbp, layer), _layer_spec(wba, layer), _layer_spec(wbl, layer),
                  _layer_spec(w_out, layer)] + gate_specs,
        out_specs=tok(d),
        out_shape=jax.ShapeDtypeStruct((n, d), F32),
        compiler_params=_params("parallel"),
        name="merge",
    )(x2, g.reshape(1, d), merge_bias, pool2, attn2, lru2, wbp, wba, wbl, w_out, *([w_in] * n_gate_blocks))


def _block_diag(w):
    g, c, _ = w.shape
    eye = jnp.eye(g, dtype=w.dtype)
    return (eye[:, None, :, None] * w[:, :, None, :]).reshape(g * c, g * c)


def kernel(x, ffn1_norm, ffn1_w_in, ffn1_w_out, mix_norm, w_in, pool_w, pool_scale, attn_lambda, attn_subln, lru_conv_w, lru_conv_b, lru_w_a, lru_b_a, lru_w_x, lru_b_x, lru_lambda, w_branch_pool, w_branch_attn, w_branch_lru, merge_bias, w_out, ffn2_norm, ffn2_w_in, ffn2_w_out, final_norm):
    b, s, d = x.shape
    depth = ffn1_norm.shape[0]
    qk = ATTN_HEADS * 2 * ATTN_QK_DIM
    mix_cols = POOL_WIDTH + 2 * qk + ATTN_WIDTH + 2 * LRU_WIDTH
    x2 = x.reshape(b * s, d)
    for l in range(depth):
        lam_init = 0.8 - 0.6 * math.exp(-0.3 * l)
        last = l == depth - 1
        x2 = _ffn(x2, ffn1_norm[l], ffn1_w_in, ffn1_w_out, l)

        p, q, k, vt, lx, lg = _inproj(x2, mix_norm[l], w_in, l, mix_cols, s)
        pool = _pool(p.reshape(b, s, -1), _block_diag(pool_w[l]).astype(BF16), pool_scale[l])
        w_gates = jnp.concatenate(
            [_block_diag(lru_w_a[l, 0]), _block_diag(lru_w_x[l, 0]),
             _block_diag(lru_w_a[l, 1]), _block_diag(lru_w_x[l, 1])], axis=1).astype(BF16)
        b_gates = jnp.concatenate([lru_b_a[l, 0], lru_b_x[l, 0], lru_b_a[l, 1], lru_b_x[l, 1]])
        lru = _lru(lx.reshape(b, s, -1), lg.reshape(b, s, -1), lru_conv_w[l], lru_conv_b[l],
                   w_gates, b_gates, lru_lambda[l])
        attn = _attn(q.reshape(b, s, -1), k.reshape(b, s, -1), vt,
                     attn_lambda[l], attn_subln[l], lam_init)
        x2 = _merge(x2, mix_norm[l], w_in, mix_cols, merge_bias[l],
                    pool.reshape(b * s, -1), attn.reshape(b * s, -1), lru.reshape(b * s, -1),
                    w_branch_pool, w_branch_attn, w_branch_lru, w_out, l)

        x2 = _ffn(x2, ffn2_norm[l], ffn2_w_in, ffn2_w_out, l, final_norm if last else None)
    return x2.reshape(b, s, d)
```

```python
import functools
import math

import jax
import jax.numpy as jnp
from jax import lax
from jax.experimental import pallas as pl
from jax.experimental.pallas import tpu as pltpu

F32 = jnp.float32
BF16 = jnp.bfloat16

NORM_EPS = 1e-6
SUBLN_EPS = 1e-5
LRU_C = 8.0

POOL_WIDTH = 256
ATTN_HEADS = 4
ATTN_QK_DIM = 64
ATTN_V_DIM = 128
ATTN_WIDTH = ATTN_HEADS * ATTN_V_DIM
LRU_WIDTH = 256
LRU_CONV_WIDTH = 4

LOG2E = math.log2(math.e)
Q_SCALE = ATTN_QK_DIM ** -0.5 * LOG2E

LANES = 128
SUBLANES = 8
MXU_COLS = 256
VMEM_LIMIT_BYTES = 52 * 1024 * 1024
FFN_VMEM_LIMIT_BYTES = 57 * 1024 * 1024


def _params(*sem, vmem_limit_bytes=VMEM_LIMIT_BYTES):
    return pltpu.CompilerParams(dimension_semantics=sem, vmem_limit_bytes=vmem_limit_bytes)


def _const_spec(shape):
    nd = len(shape)
    return pl.BlockSpec(shape, lambda *_: (0,) * nd, pipeline_mode=pl.Buffered(1))


def _rms(x, g, eps):
    ms = jnp.mean(x * x, axis=-1, keepdims=True)
    return x * lax.rsqrt(ms + eps) * g


def _dot(a, b):
    return jnp.dot(a, b, preferred_element_type=F32)


def _dot_nt(a, b):
    return lax.dot_general(a, b, (((1,), (1,)), ((), ())), preferred_element_type=F32)


def _ffn_body(*refs, d_ff, chunk, final):
    if final:
        x_ref, g_ref, win_ref, wout_ref, fg_ref, o_ref, acc_ref = refs
    else:
        x_ref, g_ref, win_ref, wout_ref, o_ref, acc_ref = refs
    x = x_ref[...]
    h = _rms(x, g_ref[...], NORM_EPS).astype(BF16)
    for c in range(d_ff // chunk):
        lo = c * chunk
        gate = _dot(h, win_ref[:, lo:lo + chunk].astype(BF16))
        up = _dot(h, win_ref[:, d_ff + lo:d_ff + lo + chunk].astype(BF16))
        act = (gate * jax.nn.sigmoid(gate) * up).astype(BF16)
        part = _dot(act, wout_ref[lo:lo + chunk, :].astype(BF16))
        if c == 0:
            acc_ref[...] = part
        else:
            acc_ref[...] += part
    y = x + 0.5 * acc_ref[...]
    if final:
        y = _rms(y, fg_ref[...], NORM_EPS)
    o_ref[...] = y


def _layer_spec(stacked, layer):
    zeros = (0,) * (stacked.ndim - 1)
    return pl.BlockSpec((None,) + stacked.shape[1:], lambda *_: (layer,) + zeros,
                        pipeline_mode=pl.Buffered(1))


def _ffn(x2, g, w_in, w_out, layer, final_g=None, *, tm=1024):
    n, d = x2.shape
    d_ff = w_out.shape[1]
    assert n % tm == 0 and d_ff % MXU_COLS == 0
    final = final_g is not None
    in_specs = [
        pl.BlockSpec((tm, d), lambda i: (i, 0)),
        _const_spec((1, d)),
        _layer_spec(w_in, layer),
        _layer_spec(w_out, layer),
    ]
    args = [x2, g.reshape(1, d), w_in, w_out]
    if final:
        in_specs.append(_const_spec((1, d)))
        args.append(final_g.reshape(1, d))
    return pl.pallas_call(
        functools.partial(_ffn_body, d_ff=d_ff, chunk=MXU_COLS, final=final),
        grid=(n // tm,),
        in_specs=in_specs,
        out_specs=pl.BlockSpec((tm, d), lambda i: (i, 0)),
        out_shape=jax.ShapeDtypeStruct((n, d), F32),
        scratch_shapes=[pltpu.VMEM((tm, d), F32)],
        compiler_params=_params("parallel", vmem_limit_bytes=FFN_VMEM_LIMIT_BYTES),
        name="ffn_final" if final else "ffn",
    )(*args)


def _inproj_body(x_ref, g_ref, w_ref, p_ref, q_ref, k_ref, vt_ref, lx_ref, lg_ref):
    h = _rms(x_ref[...], g_ref[...], NORM_EPS).astype(BF16)
    qk = ATTN_HEADS * 2 * ATTN_QK_DIM
    off = 0

    def proj(width):
        nonlocal off
        out = _dot(h, w_ref[:, off:off + width].astype(BF16))
        off += width
        return out

    p_ref[...] = proj(POOL_WIDTH)
    q_ref[...] = (proj(qk) * Q_SCALE).astype(BF16)
    k_ref[...] = proj(qk).astype(BF16)
    vt_ref[0] = proj(ATTN_WIDTH).T.astype(BF16)
    lx_ref[...] = proj(LRU_WIDTH)
    lg_ref[...] = proj(LRU_WIDTH)


def _inproj(x2, g, w, layer, mix_cols, seq, *, tm=1024):
    n, d = x2.shape
    assert seq % tm == 0 and n % seq == 0
    tiles = seq // tm
    qk = ATTN_HEADS * 2 * ATTN_QK_DIM

    def tok(width, dtype):
        return pl.BlockSpec((tm, width), lambda i: (i, 0)), jax.ShapeDtypeStruct((n, width), dtype)

    vt = (pl.BlockSpec((1, ATTN_WIDTH, tm), lambda i: (i // tiles, 0, i % tiles)),
          jax.ShapeDtypeStruct((n // seq, ATTN_WIDTH, seq), BF16))
    outs = [tok(POOL_WIDTH, F32), tok(qk, BF16), tok(qk, BF16), vt, tok(LRU_WIDTH, F32), tok(LRU_WIDTH, F32)]
    return pl.pallas_call(
        _inproj_body,
        grid=(n // tm,),
        in_specs=[pl.BlockSpec((tm, d), lambda i: (i, 0)),
                  _const_spec((1, d)),
                  pl.BlockSpec((None, d, mix_cols), lambda i: (layer, 0, 0), pipeline_mode=pl.Buffered(1))],
        out_specs=[spec for spec, _ in outs],
        out_shape=[shape for _, shape in outs],
        compiler_params=_params("parallel"),
        name="inproj",
    )(x2, g.reshape(1, d), w)


POOL_PAD = 16


def _pool_body(p_ref, w_ref, sc_ref, o_ref, pad_ref, *, seq, rows):
    zeros = jnp.zeros((POOL_PAD, POOL_WIDTH), F32)
    pad_ref[0:POOL_PAD, :] = zeros
    pad_ref[seq + POOL_PAD:seq + 2 * POOL_PAD, :] = zeros
    pad_ref[POOL_PAD:seq + POOL_PAD, :] = p_ref[0]

    low = lax.broadcasted_iota(jnp.int32, (rows, LANES), 1) < LANES // 2
    row = lax.broadcasted_iota(jnp.int32, (rows, LANES), 0)

    def doubled(a, d):
        return a[d:, :] + a[:a.shape[0] - d, :]

    def chunk(c, carry):
        t0 = pl.multiple_of(c * rows, rows)
        xs = pad_ref[pl.ds(t0, rows + 2 * POOL_PAD), :]
        xa, xb = xs[:, :LANES], xs[:, LANES:]
        a2 = doubled(xa, 1)
        a4 = doubled(a2, 2)
        b8 = doubled(doubled(doubled(xb, 1), 2), 4)
        b16 = doubled(b8, 8)
        sum_a = jnp.where(low, a2[POOL_PAD - 1:POOL_PAD - 1 + rows], a4[POOL_PAD - 2:POOL_PAD - 2 + rows])
        sum_b = jnp.where(low, b8[POOL_PAD - 4:POOL_PAD - 4 + rows], b16[POOL_PAD - 8:POOL_PAD - 8 + rows])
        t = t0 + row

        def count(half):
            return (jnp.minimum(t + half, seq) - jnp.maximum(t - half, 0)).astype(F32)

        cnt_a = jnp.where(low, count(1), count(2))
        cnt_b = jnp.where(low, count(4), count(8))
        mix_a = sum_a / cnt_a - xa[POOL_PAD:POOL_PAD + rows]
        mix_b = sum_b / cnt_b - xb[POOL_PAD:POOL_PAD + rows]
        mixed = jnp.concatenate([mix_a, mix_b], axis=1).astype(BF16)
        y = _dot(mixed, w_ref[...]) * sc_ref[...]
        o_ref[0, pl.ds(t0, rows), :] = y.astype(BF16)
        return carry

    lax.fori_loop(0, seq // rows, chunk, 0)


def _pool(p3, w_bd, scale, *, rows=512):
    b, s, c = p3.shape
    assert s % rows == 0 and c == POOL_WIDTH == 2 * LANES
    return pl.pallas_call(
        functools.partial(_pool_body, seq=s, rows=rows),
        grid=(b,),
        in_specs=[pl.BlockSpec((1, s, c), lambda i: (i, 0, 0)),
                  _const_spec(w_bd.shape),
                  _const_spec((1, c))],
        out_specs=pl.BlockSpec((1, s, c), lambda i: (i, 0, 0)),
        out_shape=jax.ShapeDtypeStruct((b, s, c), BF16),
        scratch_shapes=[pltpu.VMEM((s + 2 * POOL_PAD, c), F32)],
        compiler_params=_params("parallel"),
        name="pool",
    )(p3, w_bd, scale.reshape(1, c))


LRU_PAD = 8
SCAN_ROWS = 8


def _softplus(x):
    return jnp.maximum(x, 0.0) + jnp.log1p(jnp.exp(-jnp.abs(x)))


def _lru_body(lx_ref, lg_ref, cw_ref, cb_ref, wg_ref, bg_ref, lam_ref, o_ref,
              pad_ref, af_ref, uf_ref, ab_ref, ub_ref, *, seq, rows):
    w = LRU_WIDTH
    zeros = jnp.zeros((LRU_PAD, w), F32)
    pad_ref[0:LRU_PAD, :] = zeros
    pad_ref[seq + LRU_PAD:seq + 2 * LRU_PAD, :] = zeros
    pad_ref[LRU_PAD:seq + LRU_PAD, :] = lx_ref[0]

    decay = -LRU_C * _softplus(-lam_ref[...])
    lead = LRU_CONV_WIDTH // 2
    row_in_blk = lax.broadcasted_iota(jnp.int32, (rows, w), 0) % SCAN_ROWS

    def blk_shift(x, d, fill, reverse):
        blocks = x.reshape(rows // SCAN_ROWS, SCAN_ROWS, w)
        if reverse:
            moved = pltpu.roll(blocks, SCAN_ROWS - d, axis=1).reshape(rows, w)
            return jnp.where(row_in_blk < SCAN_ROWS - d, moved, fill)
        moved = pltpu.roll(blocks, d, axis=1).reshape(rows, w)
        return jnp.where(row_in_blk >= d, moved, fill)

    def gates(c, carry):
        t0 = pl.multiple_of(c * rows, rows)
        xs = pad_ref[pl.ds(t0, rows + 2 * LRU_PAD), :]
        xf = cb_ref[...]
        for j in range(LRU_CONV_WIDTH):
            s0 = LRU_PAD - lead + j
            xf = xf + cw_ref[j:j + 1, :] * xs[s0:s0 + rows]
        g = _dot(xf.astype(BF16), wg_ref[...]) + bg_ref[...]
        for d, (a_ref, u_ref) in enumerate(((af_ref, uf_ref), (ab_ref, ub_ref))):
            r = jax.nn.sigmoid(g[:, (2 * d) * w:(2 * d + 1) * w])
            i = jax.nn.sigmoid(g[:, (2 * d + 1) * w:(2 * d + 2) * w])
            log_a = decay[d:d + 1, :] * r
            a = jnp.exp(log_a)
            one_minus_a2 = -jnp.tanh(log_a) * (a * a + 1.0)
            u = jnp.sqrt(one_minus_a2) * (i * xf)
            step = 1
            while step < SCAN_ROWS:
                u = u + a * blk_shift(u, step, 0.0, d == 1)
                a = a * blk_shift(a, step, 1.0, d == 1)
                step *= 2
            a_ref[pl.ds(t0, rows), :] = a
            u_ref[pl.ds(t0, rows), :] = u
        return carry

    lax.fori_loop(0, seq // rows, gates, 0)

    def carry_blocks(i, carry):
        hf, hb = carry
        base_f = pl.multiple_of(i * SCAN_ROWS, SCAN_ROWS)
        base_b = pl.multiple_of(seq - SCAN_ROWS - i * SCAN_ROWS, SCAN_ROWS)
        tf = uf_ref[pl.ds(base_f, SCAN_ROWS), :] + af_ref[pl.ds(base_f, SCAN_ROWS), :] * hf
        uf_ref[pl.ds(base_f, SCAN_ROWS), :] = tf
        tb = ub_ref[pl.ds(base_b, SCAN_ROWS), :] + ab_ref[pl.ds(base_b, SCAN_ROWS), :] * hb
        ub_ref[pl.ds(base_b, SCAN_ROWS), :] = tb
        return tf[SCAN_ROWS - 1:SCAN_ROWS, :], tb[0:1, :]

    h0 = jnp.zeros((1, w), F32)
    lax.fori_loop(0, seq // SCAN_ROWS, carry_blocks, (h0, h0), unroll=8)

    def gate_out(c, carry):
        t0 = pl.multiple_of(c * rows, rows)
        h = uf_ref[pl.ds(t0, rows), :] + ub_ref[pl.ds(t0, rows), :]
        y = jax.nn.gelu(lg_ref[0, pl.ds(t0, rows), :]) * h
        o_ref[0, pl.ds(t0, rows), :] = y.astype(BF16)
        return carry

    lax.fori_loop(0, seq // rows, gate_out, 0)


def _lru(lx3, lg3, conv_w, conv_b, w_gates, b_gates, lam, *, rows=512):
    b, s, c = lx3.shape
    assert s % rows == 0 and rows % SCAN_ROWS == 0 and c == LRU_WIDTH
    seq_spec = pl.BlockSpec((1, s, c), lambda i: (i, 0, 0))
    return pl.pallas_call(
        functools.partial(_lru_body, seq=s, rows=rows),
        grid=(b,),
        in_specs=[seq_spec, seq_spec,
                  _const_spec(conv_w.shape), _const_spec((1, c)),
                  _const_spec(w_gates.shape), _const_spec((1, 4 * c)),
                  _const_spec(lam.shape)],
        out_specs=seq_spec,
        out_shape=jax.ShapeDtypeStruct((b, s, c), BF16),
        scratch_shapes=[pltpu.VMEM((s + 2 * LRU_PAD, c), F32)]
        + [pltpu.VMEM((s, c), F32) for _ in range(4)],
        compiler_params=_params("parallel"),
        name="lru",
    )(lx3, lg3, conv_w, conv_b.reshape(1, c), w_gates, b_gates.reshape(1, 4 * c), lam)


ALIBI_SPLIT = 64
SKEW = 2
SCORE_LIMIT = 32.0
NORM_MARGIN = 0.98
VALUE_LIMIT = 2.0 ** 80

def _bf16_split3(c):
    pieces = []
    for _ in range(3):
        p = c.astype(BF16).astype(F32)
        pieces.append(p)
        c = c - p
    return pieces


def _alibi_tables(seq):
    assert seq <= ALIBI_SPLIT * 256
    slopes = jnp.asarray([2.0 ** (-8.0 / ATTN_HEADS * (i + 1)) for i in range(ATTN_HEADS)], F32)
    coef = slopes * LOG2E
    pos = jnp.arange(seq)
    parts = [jnp.broadcast_to((ALIBI_SPLIT * (pos // ALIBI_SPLIT)).astype(F32), (ATTN_HEADS, seq)),
             jnp.broadcast_to((pos % ALIBI_SPLIT).astype(F32), (ATTN_HEADS, seq))]
    cs = [jnp.broadcast_to(c[:, None], (ATTN_HEADS, seq)) for c in _bf16_split3(coef)]
    q_cols = [part for part in parts for _ in cs] + [c for _ in parts for c in cs]
    k_cols = [-c for _ in parts for c in cs] + [part for part in parts for _ in cs]
    pad = ((0, 0), (0, 0), (0, LANES - len(q_cols)))
    q_tab = jnp.pad(jnp.stack(q_cols, axis=-1), pad).astype(BF16)
    k_tab = jnp.pad(jnp.stack(k_cols, axis=-1), pad).astype(BF16)
    return coef, q_tab, k_tab


def _attn_body(lam_ref, g_ref, q_ref, qall_ref, qtab_ref, dbias_ref, k_ref, ktab_ref, vt_ref, o_ref,
               acc1_ref, acc2_ref, flag_ref, *, tq, tk, seq, lam_init):
    n_diag = tq // tk
    diag = pl.program_id(2) * n_diag

    qf = q_ref[0].astype(F32)
    lane = lax.broadcasted_iota(jnp.int32, qf.shape, 1)
    q1 = jnp.where(lane < ATTN_QK_DIM, qf, 0.0).astype(BF16)
    q2 = jnp.where(lane >= ATTN_QK_DIM, qf, 0.0).astype(BF16)
    tab_before = qtab_ref[0]
    tab_after = (-tab_before.astype(F32)).astype(BF16)

    same_map = ((lax.broadcasted_iota(jnp.int32, (LANES, LANES), 0) < ATTN_QK_DIM)
                == (lax.broadcasted_iota(jnp.int32, (LANES, LANES), 1) < ATTN_QK_DIM))
    map_ones = jnp.where(same_map, 1.0, 0.0).astype(BF16)

    def max_sq_norm(x):
        return jnp.max(_dot((x * x).astype(BF16), map_ones), axis=0, keepdims=True)

    @pl.when(pl.program_id(2) == 0)
    def _():
        norms = max_sq_norm(qall_ref[0].astype(F32)) * max_sq_norm(k_ref[0].astype(F32))
        v_max = jnp.full((1, LANES), jnp.max(jnp.abs(vt_ref[0].astype(F32))), F32)
        excess = jnp.maximum(norms - SCORE_LIMIT ** 2 * NORM_MARGIN, v_max - VALUE_LIMIT)
        flag_ref[0] = (jnp.max(excess) <= 0.0).astype(jnp.int32)

    plain_softmax_ok = flag_ref[0] == 1

    lp = lam_ref[...]
    lam = (jnp.exp(jnp.sum(lp[0:1] * lp[1:2], axis=-1, keepdims=True))
           - jnp.exp(jnp.sum(lp[2:3] * lp[3:4], axis=-1, keepdims=True)) + lam_init)
    gain = g_ref[...] * (1.0 - lam_init)

    acc1_ref[...] = jnp.zeros_like(acc1_ref)
    acc2_ref[...] = jnp.zeros_like(acc2_ref)

    hd = ATTN_V_DIM

    def weighted(p, k0):
        pv = _dot(vt_ref[0, :, pl.ds(k0, tk)], p.astype(BF16))
        return jnp.concatenate([pv, jnp.sum(p.reshape(tk // SUBLANES, SUBLANES, tq), axis=0)], axis=0)

    def update(s, m, acc_ref, k0):
        m_new = jnp.maximum(m, jnp.max(s, axis=0, keepdims=True))
        acc_ref[...] = jnp.exp2(m - m_new) * acc_ref[...] + weighted(jnp.exp2(s - m_new), k0)
        return m_new

    def update_plain(s, m, acc_ref, k0):
        acc_ref[...] += weighted(jnp.exp2(s), k0)
        return m

    def folded_scores(jj):
        j = jj + n_diag * (jj >= diag).astype(jnp.int32)
        tab = jnp.where(j > diag, tab_after, tab_before)
        k0 = pl.multiple_of(j * tk, tk)
        ka = jnp.concatenate([k_ref[0, pl.ds(k0, tk), :], ktab_ref[0, pl.ds(k0, tk), :]], axis=1)
        return (_dot_nt(ka, jnp.concatenate([q1, tab], axis=1)),
                _dot_nt(ka, jnp.concatenate([q2, tab], axis=1)), k0)

    def diag_scores(r):
        k0 = pl.multiple_of((diag + r) * tk, tk)
        kc = k_ref[0, pl.ds(k0, tk), :]
        return _dot_nt(kc, q1) + dbias_ref[0, r], _dot_nt(kc, q2) + dbias_ref[0, r], k0

    def all_chunks(step_fn):
        m1 = m2 = jnp.full((1, tq), -jnp.inf, F32)
        pending = []
        for step in range(seq // tk):
            pending.append(diag_scores(step) if step < n_diag else folded_scores(jnp.int32(step - n_diag)))
            if len(pending) > SKEW:
                s1, s2, k0 = pending.pop(0)
                m1 = step_fn(s1, m1, acc1_ref, k0)
                m2 = step_fn(s2, m2, acc2_ref, k0)
        for s1, s2, k0 in pending:
            m1 = step_fn(s1, m1, acc1_ref, k0)
            m2 = step_fn(s2, m2, acc2_ref, k0)

    @pl.when(plain_softmax_ok)
    def _():
        all_chunks(update_plain)

    @pl.when(jnp.logical_not(plain_softmax_ok))
    def _():
        all_chunks(update)

    inv1 = 1.0 / jnp.sum(acc1_ref[hd:, :], axis=0, keepdims=True)
    inv2 = lam / jnp.sum(acc2_ref[hd:, :], axis=0, keepdims=True)
    ot = acc1_ref[:hd, :] * inv1 - acc2_ref[:hd, :] * inv2
    ot = ot * lax.rsqrt(jnp.mean(ot * ot, axis=0, keepdims=True) + SUBLN_EPS)
    o_ref[0] = (ot.T * gain).astype(BF16)


def _attn(q3, k3, vt3, lam_params, subln_g, lam_init, *, tq=1024, tk=1024):
    b, s, _ = q3.shape
    hd = ATTN_V_DIM
    assert tq % tk == 0 and s % tq == 0 and hd == LANES == 2 * ATTN_QK_DIM
    n_diag = tq // tk
    coef, q_tab, k_tab = _alibi_tables(s)
    key_pos = (jnp.arange(n_diag)[:, None, None] * tk + jnp.arange(tk)[None, :, None])
    dist = jnp.abs(key_pos - jnp.arange(tq)[None, None, :]).astype(F32)
    diag_bias = -coef[:, None, None, None] * dist
    k_spec = pl.BlockSpec((1, s, hd), lambda bi, hi, qi: (bi, 0, hi))
    vt_spec = pl.BlockSpec((1, hd, s), lambda bi, hi, qi: (bi, hi, 0))
    q_spec = pl.BlockSpec((1, tq, hd), lambda bi, hi, qi: (bi, qi, hi))
    return pl.pallas_call(
        functools.partial(_attn_body, tq=tq, tk=tk, seq=s, lam_init=lam_init),
        grid=(b, ATTN_HEADS, s // tq),
        in_specs=[_const_spec(lam_params.shape),
                  _const_spec((1, hd)),
                  q_spec,
                  k_spec,
                  pl.BlockSpec((1, tq, LANES), lambda bi, hi, qi: (hi, qi, 0)),
                  pl.BlockSpec((1, n_diag, tk, tq), lambda bi, hi, qi: (hi, 0, 0, 0)),
                  k_spec,
                  pl.BlockSpec((1, s, LANES), lambda bi, hi, qi: (hi, 0, 0)),
                  vt_spec],
        out_specs=q_spec,
        out_shape=jax.ShapeDtypeStruct((b, s, ATTN_WIDTH), BF16),
        scratch_shapes=[pltpu.VMEM((hd + SUBLANES, tq), F32), pltpu.VMEM((hd + SUBLANES, tq), F32),
                        pltpu.SMEM((1,), jnp.int32)],
        compiler_params=_params("parallel", "parallel", "arbitrary"),
        name="diff_attn",
    )(lam_params, subln_g.reshape(1, hd), q3, q3, q_tab, diag_bias, k3, k_tab, vt3)


def _merge_body(*refs, n_gate_blocks):
    (x_ref, g_ref, mb_ref, pool_ref, attn_ref, lru_ref, wbp_ref, wba_ref, wbl_ref, wout_ref), rest = refs[:10], refs[10:]
    wgate_refs, o_ref = rest[:n_gate_blocks], rest[n_gate_blocks]
    x = x_ref[...]
    d = x.shape[-1]
    blk = wgate_refs[0].shape[-1]
    h = _rms(x, g_ref[...], NORM_EPS).astype(BF16)

    def gate_logits(i):
        parts, lo = [], i * d
        while lo < (i + 1) * d:
            off = lo % blk
            width = min(blk - off, (i + 1) * d - lo)
            parts.append(_dot(h, wgate_refs[lo // blk][:, off:off + width].astype(BF16)))
            lo += width
        return jnp.concatenate(parts, axis=1)

    merged = None
    for i, (br_ref, wb_ref) in enumerate(((pool_ref, wbp_ref), (attn_ref, wba_ref), (lru_ref, wbl_ref))):
        logits = gate_logits(i) + mb_ref[i:i + 1, :]
        term = jax.nn.sigmoid(logits) * _dot(br_ref[...], wb_ref[...].astype(BF16))
        merged = term if merged is None else merged + term
    o_ref[...] = x + _dot(merged.astype(BF16), wout_ref[...].astype(BF16))


def _merge(x2, g, w_in, gate_col0, merge_bias, pool2, attn2, lru2, wbp, wba, wbl, w_out, layer, *, tm=1024):
    n, d = x2.shape
    blk = math.gcd(gate_col0, 3 * d)
    assert blk % MXU_COLS == 0 and n % tm == 0
    n_gate_blocks = 3 * d // blk

    def tok(width):
        return pl.BlockSpec((tm, width), lambda i: (i, 0))

    gate_specs = [pl.BlockSpec((None, d, blk), lambda i, j=j: (layer, 0, gate_col0 // blk + j),
                               pipeline_mode=pl.Buffered(1)) for j in range(n_gate_blocks)]
    return pl.pallas_call(
        functools.partial(_merge_body, n_gate_blocks=n_gate_blocks),
        grid=(n // tm,),
        in_specs=[tok(d), _const_spec((1, d)), _const_spec(merge_bias.shape),
                  tok(pool2.shape[1]), tok(attn2.shape[1]), tok(lru2.shape[1]),
                  _layer_spec(wbp, layer), _layer_spec(wba, layer), _layer_spec(wbl, layer),
                  _layer_spec(w_out, layer)] + gate_specs,
        out_specs=tok(d),
        out_shape=jax.ShapeDtypeStruct((n, d), F32),
        compiler_params=_params("parallel"),
        name="merge",
    )(x2, g.reshape(1, d), merge_bias, pool2, attn2, lru2, wbp, wba, wbl, w_out, *([w_in] * n_gate_blocks))


def _block_diag(w):
    g, c, _ = w.shape
    eye = jnp.eye(g, dtype=w.dtype)
    return (eye[:, None, :, None] * w[:, :, None, :]).reshape(g * c, g * c)


def kernel(x, ffn1_norm, ffn1_w_in, ffn1_w_out, mix_norm, w_in, pool_w, pool_scale, attn_lambda, attn_subln, lru_conv_w, lru_conv_b, lru_w_a, lru_b_a, lru_w_x, lru_b_x, lru_lambda, w_branch_pool, w_branch_attn, w_branch_lru, merge_bias, w_out, ffn2_norm, ffn2_w_in, ffn2_w_out, final_norm):
    b, s, d = x.shape
    depth = ffn1_norm.shape[0]
    qk = ATTN_HEADS * 2 * ATTN_QK_DIM
    mix_cols = POOL_WIDTH + 2 * qk + ATTN_WIDTH + 2 * LRU_WIDTH
    x2 = x.reshape(b * s, d)
    for l in range(depth):
        lam_init = 0.8 - 0.6 * math.exp(-0.3 * l)
        last = l == depth - 1
        x2 = _ffn(x2, ffn1_norm[l], ffn1_w_in, ffn1_w_out, l)

        p, q, k, vt, lx, lg = _inproj(x2, mix_norm[l], w_in, l, mix_cols, s)
        pool = _pool(p.reshape(b, s, -1), _block_diag(pool_w[l]).astype(BF16), pool_scale[l])
        w_gates = jnp.concatenate(
            [_block_diag(lru_w_a[l, 0]), _block_diag(lru_w_x[l, 0]),
             _block_diag(lru_w_a[l, 1]), _block_diag(lru_w_x[l, 1])], axis=1).astype(BF16)
        b_gates = jnp.concatenate([lru_b_a[l, 0], lru_b_x[l, 0], lru_b_a[l, 1], lru_b_x[l, 1]])
        lru = _lru(lx.reshape(b, s, -1), lg.reshape(b, s, -1), lru_conv_w[l], lru_conv_b[l],
                   w_gates, b_gates, lru_lambda[l])
        attn = _attn(q.reshape(b, s, -1), k.reshape(b, s, -1), vt,
                     attn_lambda[l], attn_subln[l], lam_init)
        x2 = _merge(x2, mix_norm[l], w_in, mix_cols, merge_bias[l],
                    pool.reshape(b * s, -1), attn.reshape(b * s, -1), lru.reshape(b * s, -1),
                    w_branch_pool, w_branch_attn, w_branch_lru, w_out, l)

        x2 = _ffn(x2, ffn2_norm[l], ffn2_w_in, ffn2_w_out, l, final_norm if last else None)
    return x2.reshape(b, s, d)
```
